```python
import jax, jax.numpy as jnp
from jax import lax
import numpy as np


D_MODEL = 1024
BATCH = 4
SEQ = 4096
DEPTH = 2

NSA_HEADS = 8
NSA_KV_HEADS = 2
HEAD_DIM = 64
HPG = NSA_HEADS // NSA_KV_HEADS
CMP_LEN = 32
CMP_STRIDE = 16
CMP_HIDDEN = 256
SEL_LEN = 64
N_SEL = 16
WINDOW = 512
Q_BLOCK = 128
NSA_WIDTH = NSA_HEADS * HEAD_DIM
CONV_CH = D_MODEL // 2
CONV_WIDTH = 31
Q_COLS = NSA_WIDTH
KV_COLS = 3 * 2 * NSA_KV_HEADS * HEAD_DIM
GATE_COLS = 3 * NSA_HEADS
IN0_COLS = Q_COLS + KV_COLS + GATE_COLS + 2 * CONV_CH
MIX0_WIDTH = NSA_WIDTH + CONV_CH
GMLP_WIDTH = D_MODEL
GMLP_GROUPS = 8
GMLP_CHUNK = 128
DENSE_FF = 2816
N_EXPERTS = 8
TOP_K = 2
EXPERT_FF = 3584
MOE_BLOCK = 256
EPS = 1e-6
N_EVEN = (DEPTH + 1) // 2
N_ODD = DEPTH // 2

kernel_name = 'hybrid_nsa_conformer_gmlp_moe'


def rms_norm(x, g):
    xf = x.astype(jnp.float32)
    y = xf * lax.rsqrt(jnp.mean(xf * xf, axis=-1, keepdims=True) + EPS)
    return (y * g.astype(jnp.float32)).astype(x.dtype)


def layer_norm(x, g, b):
    xf = x.astype(jnp.float32)
    mu = jnp.mean(xf, axis=-1, keepdims=True)
    xc = xf - mu
    var = jnp.mean(xc * xc, axis=-1, keepdims=True)
    return (xc * lax.rsqrt(var + EPS) * g.astype(jnp.float32) + b.astype(jnp.float32)).astype(x.dtype)


def masked_softmax(s, mask, axis):
    s = jnp.where(mask, s.astype(jnp.float32), -1e30)
    return jax.nn.softmax(s, axis=axis)


def swiglu(x, w_gate, w_up, w_down):
    return (jax.nn.silu(x @ w_gate) * (x @ w_up)) @ w_down


def compress_blocks(k, pe, w1, b1, w2, b2):
    S = k.shape[2]
    nc = (S - CMP_LEN) // CMP_STRIDE + 1
    idx = np.arange(nc)[:, None] * CMP_STRIDE + np.arange(CMP_LEN)[None, :]
    blocks = k[:, :, idx] + pe
    flat = blocks.reshape(blocks.shape[:3] + (CMP_LEN * HEAD_DIM,))
    return jax.nn.gelu(flat @ w1 + b1) @ w2 + b2


def nsa_attention(q, k_cmp, v_cmp, k_slc, v_slc, k_win, v_win, gates,
                  cmp_pe, cmp_w1, cmp_b1, cmp_w2, cmp_b2):
    B, G, Hg, S, hd = q.shape
    scale = hd ** -0.5
    t = jnp.arange(S)

    kc = compress_blocks(k_cmp, cmp_pe[0], cmp_w1[0], cmp_b1[0], cmp_w2[0], cmp_b2[0])
    vc = compress_blocks(v_cmp, cmp_pe[1], cmp_w1[1], cmp_b1[1], cmp_w2[1], cmp_b2[1])
    nc = kc.shape[2]
    c_end = jnp.arange(nc) * CMP_STRIDE + CMP_LEN - 1
    c_mask = c_end[None, :] <= t[:, None]
    s_c = jnp.einsum('bghtd,bgnd->bghtn', q, kc) * scale
    p_c = jnp.where(c_mask, masked_softmax(s_c, c_mask, -1), 0.0)
    o_c = jnp.einsum('bghtn,bgnd->bghtd', p_c.astype(vc.dtype), vc)

    nsb = S // SEL_LEN
    cs = np.arange(nc) * CMP_STRIDE
    js = np.arange(nsb) * SEL_LEN
    overlap = jnp.asarray(((cs[:, None] < js[None, :] + SEL_LEN) &
                           (cs[:, None] + CMP_LEN > js[None, :])).astype(np.float32))
    imp = jnp.einsum('bghtn,nj->bgtj', p_c, overlap)
    blk = jnp.arange(nsb)[None, :]
    cur = (t // SEL_LEN)[:, None]
    valid = blk <= cur
    forced = (blk == 0) | (blk == cur) | (blk == cur - 1)
    score = jnp.where(valid, imp + jnp.where(forced, 1e6, 0.0), -1e9)
    n_sel = min(N_SEL, nsb)
    _, sel = lax.top_k(score, n_sel)

    nqb = S // Q_BLOCK
    kb = k_slc.reshape(B, G, nsb, SEL_LEN, hd)
    vb = v_slc.reshape(B, G, nsb, SEL_LEN, hd)
    gather = jax.vmap(jax.vmap(lambda blocks, ids: blocks[ids]))
    q_blocks = jnp.moveaxis(q.reshape(B, G, Hg, nqb, Q_BLOCK, hd), 3, 0)
    sel_blocks = jnp.moveaxis(sel.reshape(B, G, nqb, Q_BLOCK, n_sel), 2, 0)
    t_blocks = t.reshape(nqb, Q_BLOCK)

    def sel_block(args):
        qb, sb, tb = args
        kg = gather(kb, sb)
        vg = gather(vb, sb)
        s = jnp.einsum('bghqd,bgqnld->bghqnl', qb, kg) * scale
        kpos = sb[..., None] * SEL_LEN + jnp.arange(SEL_LEN)
        mask = (kpos <= tb[None, None, :, None, None])[:, :, None]
        p = masked_softmax(s, mask, (-2, -1))
        return jnp.einsum('bghqnl,bgqnld->bghqd', p.astype(vg.dtype), vg)

    o_s = lax.map(sel_block, (q_blocks, sel_blocks, t_blocks))
    o_s = jnp.moveaxis(o_s, 0, 3).reshape(B, G, Hg, S, hd)

    kp = jnp.pad(k_win, ((0, 0), (0, 0), (WINDOW, 0), (0, 0)))
    vp = jnp.pad(v_win, ((0, 0), (0, 0), (WINDOW, 0), (0, 0)))
    widx = np.arange(nqb)[:, None] * Q_BLOCK + np.arange(Q_BLOCK + WINDOW)[None, :]
    kw = kp[:, :, widx]
    vw = vp[:, :, widx]
    qw = q.reshape(B, G, Hg, nqb, Q_BLOCK, hd)
    s_w = jnp.einsum('bghnqd,bgnkd->bghnqk', qw, kw) * scale
    kpos = jnp.asarray(widx - WINDOW)[:, None, :]
    qpos = t_blocks[:, :, None]
    w_mask = (kpos <= qpos) & (kpos > qpos - WINDOW) & (kpos >= 0)
    p_w = masked_softmax(s_w, w_mask, -1)
    o_w = jnp.einsum('bghnqk,bgnkd->bghnqd', p_w.astype(vw.dtype), vw).reshape(B, G, Hg, S, hd)

    return gates[..., 0:1] * o_c + gates[..., 1:2] * o_s + gates[..., 2:3] * o_w


def mixer_nsa_conv(u, w_in, cmp_pe, cmp_w1, cmp_b1, cmp_w2, cmp_b2,
                   conv_w, conv_b, conv_ln_g, conv_ln_b, w_out):
    B, S, _ = u.shape
    proj = u @ w_in
    q, kv, g, glu = jnp.split(proj, [Q_COLS, Q_COLS + KV_COLS, Q_COLS + KV_COLS + GATE_COLS], axis=-1)
    q = q.reshape(B, S, NSA_KV_HEADS, HPG, HEAD_DIM).transpose(0, 2, 3, 1, 4)
    kv = kv.reshape(B, S, 6, NSA_KV_HEADS, HEAD_DIM).transpose(2, 0, 3, 1, 4)
    gates = jax.nn.sigmoid(g.reshape(B, S, NSA_KV_HEADS, HPG, 3).transpose(0, 2, 3, 1, 4))
    o = nsa_attention(q, kv[0], kv[1], kv[2], kv[3], kv[4], kv[5], gates,
                      cmp_pe, cmp_w1, cmp_b1, cmp_w2, cmp_b2)
    o_attn = o.transpose(0, 3, 1, 2, 4).reshape(B, S, NSA_WIDTH)
    a, b = jnp.split(glu, 2, axis=-1)
    c = a * jax.nn.sigmoid(b)
    c = lax.conv_general_dilated(c, conv_w, window_strides=(1,), padding=[(CONV_WIDTH - 1, 0)],
                                 dimension_numbers=('NWC', 'WIO', 'NWC'),
                                 feature_group_count=CONV_CH) + conv_b
    c = jax.nn.silu(layer_norm(c, conv_ln_g, conv_ln_b))
    return jnp.concatenate([o_attn, c], axis=-1) @ w_out


def mixer_gmlp(u, w_in, ln_g, ln_b, w_s, b_s, w_out):
    B, S, _ = u.shape
    z = jax.nn.gelu(u @ w_in)
    z1, z2 = jnp.split(z, 2, axis=-1)
    z2 = layer_norm(z2, ln_g, ln_b)
    nch = S // GMLP_CHUNK
    z2 = z2.reshape(B, nch, GMLP_CHUNK, GMLP_GROUPS, GMLP_WIDTH // GMLP_GROUPS)
    causal = jnp.tril(jnp.ones((GMLP_CHUNK, GMLP_CHUNK), dtype=bool))
    ws = jnp.where(causal, w_s, 0.0)
    mixed = jnp.einsum('gts,bcsgd->bctgd', ws, z2) + b_s.T[:, :, None]
    return (z1 * mixed.reshape(B, S, GMLP_WIDTH)) @ w_out


def moe_swiglu(x, w_router, w_gate, w_up, w_down):
    B, S, D = x.shape
    h = x.reshape(-1, D)
    n = h.shape[0]
    logits = (h @ w_router).astype(jnp.float32)
    top_logit, top_e = lax.top_k(logits, TOP_K)
    top_w = jax.nn.softmax(top_logit, axis=-1)
    nk = n * TOP_K
    e_flat = top_e.reshape(-1)
    w_flat = top_w.reshape(-1)
    tok_flat = jnp.arange(nk) // TOP_K
    order = jnp.argsort(e_flat)
    e_sorted = e_flat[order]
    counts = jnp.zeros((N_EXPERTS,), jnp.int32).at[e_flat].add(1)
    starts = jnp.cumsum(counts) - counts
    padded = (counts + MOE_BLOCK - 1) // MOE_BLOCK * MOE_BLOCK
    pad_ends = jnp.cumsum(padded)
    pad_starts = pad_ends - padded
    dest = pad_starts[e_sorted] + (jnp.arange(nk) - starts[e_sorted])
    n_rows = nk + N_EXPERTS * MOE_BLOCK
    n_blocks = n_rows // MOE_BLOCK
    row_tok = jnp.full((n_rows,), n, jnp.int32).at[dest].set(tok_flat[order])
    row_w = jnp.zeros((n_rows,), jnp.float32).at[dest].set(w_flat[order])
    block_e = jnp.minimum(jnp.searchsorted(pad_ends, jnp.arange(n_blocks) * MOE_BLOCK, side='right'),
                          N_EXPERTS - 1)
    h_pad = jnp.concatenate([h, jnp.zeros((1, D), h.dtype)], axis=0)
    xb = h_pad[row_tok].reshape(n_blocks, MOE_BLOCK, D)

    def expert_block(args):
        xe, e = args
        return swiglu(xe, w_gate[e], w_up[e], w_down[e])

    yb = lax.map(expert_block, (xb, block_e)).reshape(n_rows, D)
    out = jnp.zeros((n + 1, D), jnp.float32).at[row_tok].add(yb.astype(jnp.float32) * row_w[:, None])
    return out[:n].astype(x.dtype).reshape(B, S, D)


def setup_inputs(seed: int = 0) -> dict:
    key = jax.random.key(seed)
    keys = jax.random.split(key, 32)
    D = D_MODEL
    NE = N_EVEN
    NO = N_ODD

    def nrm(i, shape, scale):
        return jax.random.normal(keys[i], shape, jnp.float32) * scale

    return {
        'x': nrm(0, (BATCH, SEQ, D), 1.0),
        'norm_mix_pre': 1.0 + nrm(1, (DEPTH, D), 0.02),
        'norm_mix_post': 1.0 + nrm(2, (DEPTH, D), 0.02),
        'norm_ffn_pre': 1.0 + nrm(3, (DEPTH, D), 0.02),
        'norm_ffn_post': 1.0 + nrm(4, (DEPTH, D), 0.02),
        'nsa_conv_w_in': nrm(5, (NE, D, IN0_COLS), D ** -0.5),
        'cmp_pe': nrm(6, (NE, 2, CMP_LEN, HEAD_DIM), 0.1),
        'cmp_w1': nrm(7, (NE, 2, CMP_LEN * HEAD_DIM, CMP_HIDDEN), (CMP_LEN * HEAD_DIM) ** -0.5),
        'cmp_b1': nrm(8, (NE, 2, CMP_HIDDEN), 0.01),
        'cmp_w2': nrm(9, (NE, 2, CMP_HIDDEN, HEAD_DIM), CMP_HIDDEN ** -0.5),
        'cmp_b2': nrm(10, (NE, 2, HEAD_DIM), 0.01),
        'conv_w': nrm(11, (NE, CONV_WIDTH, 1, CONV_CH), CONV_WIDTH ** -0.5),
        'conv_b': nrm(12, (NE, CONV_CH), 0.01),
        'conv_ln_g': 1.0 + nrm(13, (NE, CONV_CH), 0.02),
        'conv_ln_b': nrm(14, (NE, CONV_CH), 0.01),
        'nsa_conv_w_out': nrm(15, (NE, MIX0_WIDTH, D), MIX0_WIDTH ** -0.5),
        'ffn_w_gate': nrm(16, (NE, D, DENSE_FF), D ** -0.5),
        'ffn_w_up': nrm(17, (NE, D, DENSE_FF), D ** -0.5),
        'ffn_w_down': nrm(18, (NE, DENSE_FF, D), DENSE_FF ** -0.5),
        'gmlp_w_in': nrm(19, (NO, D, 2 * GMLP_WIDTH), D ** -0.5),
        'gmlp_ln_g': 1.0 + nrm(20, (NO, GMLP_WIDTH), 0.02),
        'gmlp_ln_b': nrm(21, (NO, GMLP_WIDTH), 0.01),
        'gmlp_w_s': nrm(22, (NO, GMLP_GROUPS, GMLP_CHUNK, GMLP_CHUNK), GMLP_CHUNK ** -0.5),
        'gmlp_b_s': 1.0 + nrm(23, (NO, GMLP_GROUPS, GMLP_CHUNK), 0.01),
        'gmlp_w_out': nrm(24, (NO, GMLP_WIDTH, D), GMLP_WIDTH ** -0.5),
        'moe_w_router': nrm(25, (NO, D, N_EXPERTS), D ** -0.5),
        'moe_w_gate': nrm(26, (NO, N_EXPERTS, D, EXPERT_FF), D ** -0.5),
        'moe_w_up': nrm(27, (NO, N_EXPERTS, D, EXPERT_FF), D ** -0.5),
        'moe_w_down': nrm(28, (NO, N_EXPERTS, EXPERT_FF, D), EXPERT_FF ** -0.5),
    }


def reference(x, norm_mix_pre, norm_mix_post, norm_ffn_pre, norm_ffn_post,
              nsa_conv_w_in, cmp_pe, cmp_w1, cmp_b1, cmp_w2, cmp_b2,
              conv_w, conv_b, conv_ln_g, conv_ln_b, nsa_conv_w_out,
              ffn_w_gate, ffn_w_up, ffn_w_down,
              gmlp_w_in, gmlp_ln_g, gmlp_ln_b, gmlp_w_s, gmlp_b_s, gmlp_w_out,
              moe_w_router, moe_w_gate, moe_w_up, moe_w_down):
    h = x
    for layer in range(DEPTH):
        i = layer // 2
        u = rms_norm(h, norm_mix_pre[layer])
        if layer % 2 == 0:
            m = mixer_nsa_conv(u, nsa_conv_w_in[i], cmp_pe[i], cmp_w1[i], cmp_b1[i], cmp_w2[i], cmp_b2[i],
                               conv_w[i], conv_b[i], conv_ln_g[i], conv_ln_b[i], nsa_conv_w_out[i])
        else:
            m = mixer_gmlp(u, gmlp_w_in[i], gmlp_ln_g[i], gmlp_ln_b[i], gmlp_w_s[i], gmlp_b_s[i], gmlp_w_out[i])
        h = h + rms_norm(m, norm_mix_post[layer])
        u = rms_norm(h, norm_ffn_pre[layer])
        if layer % 2 == 0:
            f = swiglu(u, ffn_w_gate[i], ffn_w_up[i], ffn_w_down[i])
        else:
            f = moe_swiglu(u, moe_w_router[i], moe_w_gate[i], moe_w_up[i], moe_w_down[i])
        h = h + rms_norm(f, norm_ffn_post[layer])
    return h
```

```python
import functools

import numpy as np
import jax
import jax.numpy as jnp
from jax import lax
from jax.experimental import pallas as pl
from jax.experimental.pallas import tpu as pltpu

F32 = jnp.float32
BF16 = jnp.bfloat16

D_MODEL = 1024
N_HEADS = 8
N_KV = 2
HPG = N_HEADS // N_KV
HEAD_DIM = 64
CMP_LEN = 32
CMP_STRIDE = 16
CMP_HIDDEN = 256
SEL_LEN = 64
N_SEL = 16
WINDOW = 512
CONV_CH = D_MODEL // 2
CONV_WIDTH = 31
GMLP_GROUPS = 8
GMLP_CHUNK = 128
N_EXPERTS = 8
EPS = 1e-6
NEG = -1e30

LANES = 128
TQ = 128
TK_SEL = 512
TM_PROJ = 512
TS_CONV = 512
CONV_HALO = 32
TM_FFN = 512
TF_FFN = 1408
TM_MOE = 512
TF_MOE = 1792
TC_COMB = 256
VMEM_LIMIT = 56 * 1024 * 1024


def _params(sem):
    return pltpu.CompilerParams(dimension_semantics=sem, vmem_limit_bytes=VMEM_LIMIT)


def _dot(a, b):
    return jnp.dot(a, b, preferred_element_type=F32)


def _dot_nt(a, b):
    return lax.dot_general(a, b, (((1,), (1,)), ((), ())), preferred_element_type=F32)


def _rms(x, g):
    return x * lax.rsqrt(jnp.mean(x * x, axis=-1, keepdims=True) + EPS) * g


def _layer_norm(x, g, b):
    mu = jnp.mean(x, axis=-1, keepdims=True)
    xc = x - mu
    var = jnp.mean(xc * xc, axis=-1, keepdims=True)
    return xc * lax.rsqrt(var + EPS) * g + b


def _sigmoid(x):
    return 1.0 / (1.0 + jnp.exp(-x))


def _silu(x):
    return x * _sigmoid(x)


def _gelu_tanh(x):
    c = np.float32(np.sqrt(2.0 / np.pi))
    return 0.5 * x * (1.0 + jnp.tanh(c * (x + 0.044715 * (x * x * x))))


def _proj0_kernel(h_ref, g_ref, wq_ref, wkv_ref, wg_ref, wglu_ref,
                  q_ref, kv_ref, gates_ref, c_ref):
    u = _rms(h_ref[...], g_ref[...]).astype(BF16)
    q_ref[...] = _dot(u, wq_ref[...]).astype(BF16)
    kv_ref[...] = _dot(u, wkv_ref[...]).astype(BF16)
    gates_ref[...] = _sigmoid(_dot(u, wg_ref[...]))
    glu = _dot(u, wglu_ref[...])
    c_ref[...] = glu[:, :CONV_CH] * _sigmoid(glu[:, CONV_CH:])


def _proj0(h, g, wq, wkv, wg, wglu):
    n, d = h.shape
    tm = TM_PROJ
    full = lambda a: pl.BlockSpec(a.shape, lambda i: (0,) * a.ndim)
    row = lambda c: pl.BlockSpec((tm, c), lambda i: (i, 0))
    return pl.pallas_call(
        _proj0_kernel,
        grid=(n // tm,),
        in_specs=[row(d), full(g), full(wq), full(wkv), full(wg), full(wglu)],
        out_specs=[row(wq.shape[1]), row(wkv.shape[1]), row(wg.shape[1]), row(CONV_CH)],
        out_shape=[jax.ShapeDtypeStruct((n, wq.shape[1]), BF16),
                   jax.ShapeDtypeStruct((n, wkv.shape[1]), BF16),
                   jax.ShapeDtypeStruct((n, wg.shape[1]), F32),
                   jax.ShapeDtypeStruct((n, CONV_CH), F32)],
        compiler_params=_params(("parallel",)),
        name="proj0",
    )(h, g, wq, wkv, wg, wglu)


def _compress_kernel(kr_ref, pe_ref, w1_ref, b1_ref, w2_ref, b2_ref, o_ref):
    half = CMP_STRIDE * HEAD_DIM
    kr = kr_ref[0, 0].astype(F32)
    pe = pe_ref[0]
    top = (kr + pe[:, :half]).astype(BF16)
    bot = (kr + pe[:, half:]).astype(BF16)
    a = _dot(top, w1_ref[0, :half, :])
    b = _dot(bot, w1_ref[0, half:, :])
    nrow = b.shape[0]
    pre = a + pltpu.roll(b, nrow - 1, 0) + b1_ref[0]
    hid = _gelu_tanh(pre).astype(BF16)
    o_ref[0, 0] = (_dot(hid, w2_ref[0]) + b2_ref[0]).astype(o_ref.dtype)


def _compress(kr, pe, w1, b1, w2, b2):
    _, bg, nr, feat = kr.shape
    return pl.pallas_call(
        _compress_kernel,
        grid=(2, bg),
        in_specs=[pl.BlockSpec((1, 1, nr, feat), lambda j, i: (j, i, 0, 0)),
                  pl.BlockSpec((1, 1, 2 * feat), lambda j, i: (j, 0, 0)),
                  pl.BlockSpec((1, 2 * feat, CMP_HIDDEN), lambda j, i: (j, 0, 0)),
                  pl.BlockSpec((1, 1, CMP_HIDDEN), lambda j, i: (j, 0, 0)),
                  pl.BlockSpec((1, CMP_HIDDEN, HEAD_DIM), lambda j, i: (j, 0, 0)),
                  pl.BlockSpec((1, 1, HEAD_DIM), lambda j, i: (j, 0, 0))],
        out_specs=pl.BlockSpec((1, 1, nr, HEAD_DIM), lambda j, i: (j, i, 0, 0)),
        out_shape=jax.ShapeDtypeStruct((2, bg, nr, HEAD_DIM), BF16),
        compiler_params=_params(("parallel", "parallel")),
        name="compress",
    )(kr, pe, w1, b1, w2, b2)


def _attn_kernel(q_ref, kc_ref, vc_ref, ks_ref, vs_ref, kw_ref, vw_ref, gates_ref,
                 ov_ref, e_ref, place_ref, o_ref):
    qi = pl.program_id(1)
    q0 = qi * TQ
    nsb = SEL_LEN
    ncp = kc_ref.shape[1]
    gates = gates_ref[0]
    outs = []
    for g in range(N_KV):
        qs = jnp.concatenate(
            [q_ref[0, :, LANES * (HPG * g + h):LANES * (HPG * g + h + 1)] for h in range(HPG)], axis=0)

        s = _dot_nt(qs, kc_ref[0]).reshape(HPG, TQ, ncp)
        n_idx = lax.broadcasted_iota(jnp.int32, (TQ, ncp), 1)
        t_idx = q0 + lax.broadcasted_iota(jnp.int32, (TQ, ncp), 0)
        cmask = (n_idx * CMP_STRIDE + (CMP_LEN - 1) <= t_idx)[None]
        s = jnp.where(cmask, s, NEG)
        m = jnp.max(s, axis=-1, keepdims=True)
        p = jnp.where(cmask, jnp.exp(s - m), 0.0)
        l = jnp.sum(p, axis=-1, keepdims=True)
        p = p / jnp.where(l > 0.0, l, 1.0)
        o_c = _dot(p.reshape(HPG * TQ, ncp).astype(BF16), vc_ref[0])
        psum = p[0] + p[1] + p[2] + p[3]
        imp = _dot(psum.astype(BF16), ov_ref[...])

        imp_t = imp.T[:nsb]
        j_idx = lax.broadcasted_iota(jnp.int32, (nsb, TQ), 0)
        cur = (q0 + lax.broadcasted_iota(jnp.int32, (nsb, TQ), 1)) // SEL_LEN
        valid = j_idx <= cur
        forced = (j_idx == 0) | (j_idx == cur) | (j_idx == cur - 1)
        score = jnp.where(valid, imp_t + jnp.where(forced, 1e6, 0.0), -1e9)
        sub = lax.broadcasted_iota(jnp.int32, (8, TQ), 0)
        chunks = [score[8 * v:8 * v + 8] for v in range(nsb // 8)]
        counts = [jnp.zeros((8, TQ), F32) for _ in range(nsb // 8)]
        for i in range(nsb):
            row = jnp.broadcast_to(score[i:i + 1], (8, TQ))
            for v in range(nsb // 8):
                if i < 8 * v:
                    ahead = row >= chunks[v]
                elif i >= 8 * v + 8:
                    ahead = row > chunks[v]
                else:
                    ahead = jnp.where(sub > (i - 8 * v),
                                      jnp.where(row >= chunks[v], 1.0, 0.0),
                                      jnp.where(row > chunks[v], 1.0, 0.0)) > 0.5
                counts[v] = counts[v] + jnp.where(ahead, 1.0, 0.0)
        sel_t = jnp.concatenate(
            [jnp.where(c < float(N_SEL), 1.0, 0.0) for c in counts] + [jnp.zeros((LANES - nsb, TQ), F32)],
            axis=0)
        sel = sel_t.T.astype(BF16)

        def sel_body(kt, carry):
            m_i, l_i, acc = carry
            k0 = pl.multiple_of(kt * TK_SEL, TK_SEL)
            kblk = ks_ref[0, pl.ds(k0, TK_SEL), :]
            vblk = vs_ref[0, pl.ds(k0, TK_SEL), :]
            sc = _dot_nt(qs, kblk).reshape(HPG, TQ, TK_SEL)
            chosen = _dot(sel, e_ref[:, pl.ds(k0, TK_SEL)])
            kpos = k0 + lax.broadcasted_iota(jnp.int32, (TQ, TK_SEL), 1)
            tpos = q0 + lax.broadcasted_iota(jnp.int32, (TQ, TK_SEL), 0)
            ok = ((chosen > 0.5) & (kpos <= tpos))[None]
            sc = jnp.where(ok, sc, NEG)
            m_new = jnp.maximum(m_i, jnp.max(sc, axis=-1, keepdims=True))
            alpha = jnp.exp(m_i - m_new)
            pexp = jnp.exp(sc - m_new)
            l_new = alpha * l_i + jnp.sum(pexp, axis=-1, keepdims=True)
            pv = _dot(pexp.reshape(HPG * TQ, TK_SEL).astype(BF16), vblk).reshape(HPG, TQ, LANES)
            return m_new, l_new, alpha * acc + pv

        init = (jnp.full((HPG, TQ, 1), NEG, F32), jnp.zeros((HPG, TQ, 1), F32),
                jnp.zeros((HPG, TQ, LANES), F32))
        n_kt = (q0 + TQ + TK_SEL - 1) // TK_SEL
        _, l_s, acc_s = lax.fori_loop(0, n_kt, sel_body, init)
        o_s = acc_s / l_s

        m_w, l_w, acc_w = init
        for d in range(WINDOW // TQ + 1):
            kt = qi - d
            k0 = pl.multiple_of(jnp.maximum(kt, 0) * TQ, TQ)
            kblk = kw_ref[0, pl.ds(k0, TQ), :]
            vblk = vw_ref[0, pl.ds(k0, TQ), :]
            sc = _dot_nt(qs, kblk).reshape(HPG, TQ, TQ)
            kpos = kt * TQ + lax.broadcasted_iota(jnp.int32, (TQ, TQ), 1)
            tpos = q0 + lax.broadcasted_iota(jnp.int32, (TQ, TQ), 0)
            ok = ((kpos <= tpos) & (kpos > tpos - WINDOW) & (kpos >= 0))[None]
            sc = jnp.where(ok, sc, NEG)
            m_new = jnp.maximum(m_w, jnp.max(sc, axis=-1, keepdims=True))
            alpha = jnp.exp(m_w - m_new)
            pexp = jnp.exp(sc - m_new)
            l_w = alpha * l_w + jnp.sum(pexp, axis=-1, keepdims=True)
            pv = _dot(pexp.reshape(HPG * TQ, TQ).astype(BF16), vblk).reshape(HPG, TQ, LANES)
            acc_w = alpha * acc_w + pv
            m_w = m_new
        o_w = acc_w / l_w

        o_c = o_c.reshape(HPG, TQ, LANES)
        gated = []
        for h in range(HPG):
            c0 = LANES * g + 3 * h
            og = (gates[:, c0:c0 + 1] * o_c[h] + gates[:, c0 + 1:c0 + 2] * o_s[h]
                  + gates[:, c0 + 2:c0 + 3] * o_w[h])
            gated.append(og.astype(BF16))
        outs.append(_dot(jnp.concatenate(gated, axis=1), place_ref[g]))
    o_ref[0] = jnp.concatenate(outs, axis=1).astype(o_ref.dtype)


def _attention(qpad, kc, vc, kv, gates, ov, emat, place):
    b, s, _ = qpad.shape
    ncp = kc.shape[1]
    kvspec = lambda j: pl.BlockSpec((1, s, LANES), lambda bi, qi: (bi, 0, j))
    return pl.pallas_call(
        _attn_kernel,
        grid=(b, s // TQ),
        in_specs=[pl.BlockSpec((1, TQ, N_HEADS * LANES), lambda bi, qi: (bi, qi, 0)),
                  pl.BlockSpec((1, ncp, LANES), lambda bi, qi: (bi, 0, 0)),
                  pl.BlockSpec((1, ncp, LANES), lambda bi, qi: (bi, 0, 0)),
                  kvspec(2), kvspec(3), kvspec(4), kvspec(5),
                  pl.BlockSpec((1, TQ, N_KV * LANES), lambda bi, qi: (bi, qi, 0)),
                  pl.BlockSpec(ov.shape, lambda bi, qi: (0, 0)),
                  pl.BlockSpec(emat.shape, lambda bi, qi: (0, 0)),
                  pl.BlockSpec(place.shape, lambda bi, qi: (0, 0, 0))],
        out_specs=pl.BlockSpec((1, TQ, N_HEADS * HEAD_DIM), lambda bi, qi: (bi, qi, 0)),
        out_shape=jax.ShapeDtypeStruct((b, s, N_HEADS * HEAD_DIM), BF16),
        compiler_params=_params(("parallel", "arbitrary")),
        name="nsa_attention",
    )(qpad, kc, vc, kv, kv, kv, kv, gates, ov, emat, place)


def _conv_kernel(c_ref, halo_ref, cw_ref, cb_ref, g_ref, b_ref, o_ref, xs_ref):
    i = pl.program_id(1)
    ts = c_ref.shape[1]
    xs_ref[0:CONV_HALO, :] = jnp.where(i > 0, halo_ref[0], 0.0)
    xs_ref[CONV_HALO:CONV_HALO + ts, :] = c_ref[0]
    acc = jnp.zeros((ts, CONV_CH), F32)
    for w in range(CONV_WIDTH):
        off = CONV_HALO - (CONV_WIDTH - 1) + w
        acc = acc + xs_ref[off:off + ts, :] * cw_ref[w:w + 1, :]
    y = _layer_norm(acc + cb_ref[...], g_ref[...], b_ref[...])
    o_ref[0] = _silu(y).astype(o_ref.dtype)


def _conv_branch(c, cw, cb, ln_g, ln_b):
    b, s, ch = c.shape
    ts = TS_CONV
    per = ts // CONV_HALO
    vec = pl.BlockSpec((1, ch), lambda bi, i: (0, 0))
    return pl.pallas_call(
        _conv_kernel,
        grid=(b, s // ts),
        in_specs=[pl.BlockSpec((1, ts, ch), lambda bi, i: (bi, i, 0)),
                  pl.BlockSpec((1, CONV_HALO, ch), lambda bi, i: (bi, jnp.maximum(i * per - 1, 0), 0)),
                  pl.BlockSpec((CONV_WIDTH, ch), lambda bi, i: (0, 0)), vec, vec, vec],
        out_specs=pl.BlockSpec((1, ts, ch), lambda bi, i: (bi, i, 0)),
        out_shape=jax.ShapeDtypeStruct((b, s, ch), BF16),
        scratch_shapes=[pltpu.VMEM((CONV_HALO + ts, ch), F32)],
        compiler_params=_params(("parallel", "arbitrary")),
        name="conv_branch",
    )(c, c, cw, cb, ln_g, ln_b)


def _outproj_kernel(oa_ref, cc_ref, wt_ref, wb_ref, h_ref, gpost_ref, gnext_ref, h1_ref, u_ref):
    m = _dot(oa_ref[...], wt_ref[...]) + _dot(cc_ref[...], wb_ref[...])
    h1 = h_ref[...] + _rms(m, gpost_ref[...])
    h1_ref[...] = h1
    u_ref[...] = _rms(h1, gnext_ref[...]).astype(u_ref.dtype)


def _outproj(oa, cc, wt, wb, h, gpost, gnext):
    n, d = h.shape
    tm = TM_PROJ
    full = lambda a: pl.BlockSpec(a.shape, lambda i: (0,) * a.ndim)
    row = lambda c: pl.BlockSpec((tm, c), lambda i: (i, 0))
    return pl.pallas_call(
        _outproj_kernel,
        grid=(n // tm,),
        in_specs=[row(oa.shape[1]), row(cc.shape[1]), full(wt), full(wb), row(d), full(gpost), full(gnext)],
        out_specs=[row(d), row(d)],
        out_shape=[jax.ShapeDtypeStruct((n, d), F32), jax.ShapeDtypeStruct((n, d), BF16)],
        compiler_params=_params(("parallel",)),
        name="outproj",
    )(oa, cc, wt, wb, h, gpost, gnext)


def _ffn_kernel(u_ref, wg_ref, wu_ref, wd_ref, h_ref, gpost_ref, gnext_ref, h2_ref, u2_ref, acc_ref):
    f = pl.program_id(1)

    @pl.when(f == 0)
    def _():
        acc_ref[...] = jnp.zeros_like(acc_ref)

    u = u_ref[...]
    act = (_silu(_dot(u, wg_ref[...])) * _dot(u, wu_ref[...])).astype(BF16)
    acc_ref[...] += _dot(act, wd_ref[...])

    @pl.when(f == pl.num_programs(1) - 1)
    def _():
        h2 = h_ref[...] + _rms(acc_ref[...], gpost_ref[...])
        h2_ref[...] = h2
        u2_ref[...] = _rms(h2, gnext_ref[...]).astype(u2_ref.dtype)


def _ffn(u, wg, wu, wd, h, gpost, gnext):
    n, d = h.shape
    ff = wg.shape[1]
    tm, tf = TM_FFN, TF_FFN
    row = pl.BlockSpec((tm, d), lambda i, f: (i, 0))
    vec = pl.BlockSpec((1, d), lambda i, f: (0, 0))
    return pl.pallas_call(
        _ffn_kernel,
        grid=(n // tm, ff // tf),
        in_specs=[row, pl.BlockSpec((d, tf), lambda i, f: (0, f)), pl.BlockSpec((d, tf), lambda i, f: (0, f)),
                  pl.BlockSpec((tf, d), lambda i, f: (f, 0)), row, vec, vec],
        out_specs=[row, row],
        out_shape=[jax.ShapeDtypeStruct((n, d), F32), jax.ShapeDtypeStruct((n, d), BF16)],
        scratch_shapes=[pltpu.VMEM((tm, d), F32)],
        compiler_params=_params(("parallel", "arbitrary")),
        name="dense_ffn",
    )(u, wg, wu, wd, h, gpost, gnext)


def _gmlp_kernel(u_ref, win_ref, lng_ref, lnb_ref, ws_ref, bs_ref, wout_ref, h_ref, gpost_ref, gnext_ref,
                 wrh_ref, wrl_ref, h3_ref, u4_ref, route_ref, mix_ref):
    tm = u_ref.shape[0]
    width = wout_ref.shape[0]
    z = _gelu_tanh(_dot(u_ref[...], win_ref[...]))
    z1 = z[:, :width]
    z2 = _layer_norm(z[:, width:], lng_ref[...], lnb_ref[...]).astype(BF16)
    r_idx = lax.broadcasted_iota(jnp.int32, (GMLP_CHUNK, GMLP_CHUNK), 0)
    c_idx = lax.broadcasted_iota(jnp.int32, (GMLP_CHUNK, GMLP_CHUNK), 1)
    gw = width // GMLP_GROUPS
    for g in range(GMLP_GROUPS):
        wsg = jnp.where(c_idx <= r_idx, ws_ref[g], 0.0).astype(BF16)
        for c in range(tm // GMLP_CHUNK):
            rows = slice(c * GMLP_CHUNK, (c + 1) * GMLP_CHUNK)
            cols = slice(g * gw, (g + 1) * gw)
            mix_ref[rows, cols] = _dot(wsg, z2[rows, cols]) + bs_ref[:, cols]
    gated = (z1 * mix_ref[...]).astype(BF16)
    y = _dot(gated, wout_ref[...])
    h3 = h_ref[...] + _rms(y, gpost_ref[...])
    h3_ref[...] = h3
    u4 = _rms(h3, gnext_ref[...])
    u4_ref[...] = u4

    u_hi = u4.astype(BF16)
    u_lo = (u4 - u_hi.astype(F32)).astype(BF16)
    logits = _dot(u_hi, wrh_ref[...]) + (_dot(u_lo, wrh_ref[...]) + _dot(u_hi, wrl_ref[...]))
    lane = lax.broadcasted_iota(jnp.int32, (tm, LANES), 1).astype(F32)
    lg = jnp.where(lane < float(N_EXPERTS), logits, NEG)
    m1 = jnp.max(lg, axis=-1, keepdims=True)
    i1 = jnp.min(jnp.where(lg == m1, lane, float(LANES)), axis=-1, keepdims=True)
    lg2 = jnp.where(lane == i1, NEG, lg)
    m2 = jnp.max(lg2, axis=-1, keepdims=True)
    i2 = jnp.min(jnp.where(lg2 == m2, lane, float(LANES)), axis=-1, keepdims=True)
    e2 = jnp.exp(m2 - m1)
    den = 1.0 + e2
    route_ref[...] = jnp.where(lane == 0.0, i1, jnp.where(lane == 1.0, i2,
                               jnp.where(lane == 2.0, 1.0 / den, jnp.where(lane == 3.0, e2 / den, 0.0))))


def _gmlp(u, win, lng, lnb, ws, bsb, wout, h, gpost, gnext, wrh, wrl):
    n, d = h.shape
    tm = TM_PROJ
    full = lambda a: pl.BlockSpec(a.shape, lambda i: (0,) * a.ndim)
    row = lambda c: pl.BlockSpec((tm, c), lambda i: (i, 0))
    return pl.pallas_call(
        _gmlp_kernel,
        grid=(n // tm,),
        in_specs=[row(d), full(win), full(lng), full(lnb), full(ws), full(bsb), full(wout), row(d),
                  full(gpost), full(gnext), full(wrh), full(wrl)],
        out_specs=[row(d), row(d), row(LANES)],
        out_shape=[jax.ShapeDtypeStruct((n, d), F32), jax.ShapeDtypeStruct((n, d), F32),
                   jax.ShapeDtypeStruct((n, LANES), F32)],
        scratch_shapes=[pltpu.VMEM((tm, wout.shape[0]), F32)],
        compiler_params=_params(("parallel",)),
        name="gmlp_router",
    )(u, win, lng, lnb, ws, bsb, wout, h, gpost, gnext, wrh, wrl)


def _expert_kernel(tile_e_ref, nused_ref, rowtok_ref, x_hbm, wg_ref, wu_ref, wd_ref, y_ref,
                   xg_ref, xb_ref, acc_ref, sem):
    i = pl.program_id(0)
    f = pl.program_id(1)
    tm = xg_ref.shape[0]
    active = i < nused_ref[0]

    def row_copy(tok, r):
        return pltpu.make_async_copy(x_hbm.at[pl.ds(tok, 1)], xg_ref.at[pl.ds(r, 1)], sem)

    @pl.when(active & (f == 0))
    def _():
        def issue(r, carry):
            row_copy(rowtok_ref[i * tm + r], r).start()
            return carry

        lax.fori_loop(0, tm, issue, 0)

        def wait(r, carry):
            row_copy(0, r).wait()
            return carry

        lax.fori_loop(0, tm, wait, 0)
        xb_ref[...] = xg_ref[...].astype(BF16)
        acc_ref[...] = jnp.zeros_like(acc_ref)

    @pl.when(active)
    def _():
        x = xb_ref[...]
        act = (_silu(_dot(x, wg_ref[0])) * _dot(x, wu_ref[0])).astype(BF16)
        acc_ref[...] += _dot(act, wd_ref[0])

    last = f == pl.num_programs(1) - 1

    @pl.when(active & last)
    def _():
        y_ref[...] = acc_ref[...]

    @pl.when(jnp.logical_not(active) & last)
    def _():
        y_ref[...] = jnp.zeros_like(y_ref)


def _experts(tile_e, nused, rowtok, x, wg, wu, wd):
    n_rows = rowtok.shape[0]
    d = x.shape[1]
    ff = wg.shape[2]
    tm, tf = TM_MOE, TF_MOE
    nf = ff // tf

    def fidx(i, f, te, nu):
        return jnp.where(i < nu[0], f, nf - 1)

    grid_spec = pltpu.PrefetchScalarGridSpec(
        num_scalar_prefetch=3,
        grid=(n_rows // tm, nf),
        in_specs=[pl.BlockSpec(memory_space=pl.ANY),
                  pl.BlockSpec((1, d, tf), lambda i, f, te, nu, rt: (te[i], 0, fidx(i, f, te, nu))),
                  pl.BlockSpec((1, d, tf), lambda i, f, te, nu, rt: (te[i], 0, fidx(i, f, te, nu))),
                  pl.BlockSpec((1, tf, d), lambda i, f, te, nu, rt: (te[i], fidx(i, f, te, nu), 0))],
        out_specs=pl.BlockSpec((tm, d), lambda i, f, te, nu, rt: (i, 0)),
        scratch_shapes=[pltpu.VMEM((tm, d), F32), pltpu.VMEM((tm, d), BF16), pltpu.VMEM((tm, d), F32),
                        pltpu.SemaphoreType.DMA],
    )
    return pl.pallas_call(
        _expert_kernel,
        grid_spec=grid_spec,
        out_shape=jax.ShapeDtypeStruct((n_rows, d), F32),
        compiler_params=_params(("arbitrary", "arbitrary")),
        name="moe_experts",
    )(tile_e, nused, rowtok, x, wg, wu, wd)


def _combine_kernel(dest_ref, y_hbm, route_ref, h_ref, gpost_ref, o_ref, ya_ref, yb_ref, sem):
    i = pl.program_id(0)
    tc = ya_ref.shape[0]

    def copies(r):
        p = 2 * (i * tc + r)
        return (pltpu.make_async_copy(y_hbm.at[pl.ds(dest_ref[p], 1)], ya_ref.at[pl.ds(r, 1)], sem.at[0]),
                pltpu.make_async_copy(y_hbm.at[pl.ds(dest_ref[p + 1], 1)], yb_ref.at[pl.ds(r, 1)], sem.at[1]))

    def issue(r, carry):
        ca, cb = copies(r)
        ca.start()
        cb.start()
        return carry

    lax.fori_loop(0, tc, issue, 0)

    def wait(r, carry):
        ca, cb = copies(r)
        ca.wait()
        cb.wait()
        return carry

    lax.fori_loop(0, tc, wait, 0)
    route = route_ref[...]
    moe = route[:, 2:3] * ya_ref[...] + route[:, 3:4] * yb_ref[...]
    o_ref[...] = h_ref[...] + _rms(moe, gpost_ref[...])


def _combine(dest, y, route, h, gpost):
    n, d = h.shape
    tc = TC_COMB
    grid_spec = pltpu.PrefetchScalarGridSpec(
        num_scalar_prefetch=1,
        grid=(n // tc,),
        in_specs=[pl.BlockSpec(memory_space=pl.ANY),
                  pl.BlockSpec((tc, LANES), lambda i, ds: (i, 0)),
                  pl.BlockSpec((tc, d), lambda i, ds: (i, 0)),
                  pl.BlockSpec((1, d), lambda i, ds: (0, 0))],
        out_specs=pl.BlockSpec((tc, d), lambda i, ds: (i, 0)),
        scratch_shapes=[pltpu.VMEM((tc, d), F32), pltpu.VMEM((tc, d), F32), pltpu.SemaphoreType.DMA((2,))],
    )
    return pl.pallas_call(
        _combine_kernel,
        grid_spec=grid_spec,
        out_shape=jax.ShapeDtypeStruct((n, d), F32),
        compiler_params=_params(("arbitrary",)),
        name="moe_combine",
    )(dest, y, route, h, gpost)


def _overlap_table(seq):
    nc = (seq - CMP_LEN) // CMP_STRIDE + 1
    ncp = seq // CMP_STRIDE
    nsb = seq // SEL_LEN
    cs = np.arange(ncp)[:, None] * CMP_STRIDE
    js = np.arange(LANES)[None, :] * SEL_LEN
    ov = (cs < js + SEL_LEN) & (cs + CMP_LEN > js) & (np.arange(ncp)[:, None] < nc) & (np.arange(LANES)[None, :] < nsb)
    return jnp.asarray(ov.astype(np.float32), BF16)


def _expand_table(seq):
    e = (np.arange(seq)[None, :] // SEL_LEN) == np.arange(LANES)[:, None]
    return jnp.asarray(e.astype(np.float32), BF16)


def _place_table():
    p = np.zeros((N_KV, HPG * LANES, HPG * HEAD_DIM), np.float32)
    for g in range(N_KV):
        for h in range(HPG):
            for dd in range(HEAD_DIM):
                p[g, h * LANES + g * HEAD_DIM + dd, h * HEAD_DIM + dd] = 1.0
    return jnp.asarray(p, BF16)


def _layer0_weights(w_in):
    d = w_in.shape[0]
    qc = N_HEADS * HEAD_DIM
    kvc = 6 * N_KV * HEAD_DIM
    gc = 3 * N_HEADS
    wq = (w_in[:, :qc] * (HEAD_DIM ** -0.5)).reshape(d, N_KV, HPG, HEAD_DIM)
    zeros = jnp.zeros_like(wq[:, 0])
    wq_pad = jnp.stack([jnp.concatenate([wq[:, 0], zeros], axis=-1),
                        jnp.concatenate([zeros, wq[:, 1]], axis=-1)], axis=1)
    wq_pad = wq_pad.reshape(d, N_HEADS * LANES).astype(BF16)
    wkv = w_in[:, qc:qc + kvc].astype(BF16)
    wg = w_in[:, qc + kvc:qc + kvc + gc].reshape(d, N_KV, 3 * HPG)
    wg_pad = jnp.pad(wg, ((0, 0), (0, 0), (0, LANES - 3 * HPG))).reshape(d, N_KV * LANES).astype(BF16)
    wglu = w_in[:, qc + kvc + gc:].astype(BF16)
    return wq_pad, wkv, wg_pad, wglu


def _dispatch_plan(route, n_tok):
    tm = TM_MOE
    e_flat = route[:, 0:2].astype(jnp.int32).reshape(-1)
    onehot = (e_flat[:, None] == jnp.arange(N_EXPERTS, dtype=jnp.int32)[None, :]).astype(jnp.int32)
    rank = jnp.sum((jnp.cumsum(onehot, axis=0) - onehot) * onehot, axis=1)
    counts = jnp.sum(onehot, axis=0)
    padded = (counts + tm - 1) // tm * tm
    pad_ends = jnp.cumsum(padded)
    pad_starts = pad_ends - padded
    dest = (pad_starts[e_flat] + rank).astype(jnp.int32)
    n_rows = 2 * n_tok + N_EXPERTS * tm
    n_tiles = n_rows // tm
    rowtok = jnp.zeros((n_rows,), jnp.int32).at[dest].set(jnp.arange(2 * n_tok, dtype=jnp.int32) // 2)
    nused = (pad_ends[-1] // tm).astype(jnp.int32)
    tiles = jnp.minimum(jnp.arange(n_tiles, dtype=jnp.int32), nused - 1) * tm
    tile_e = jnp.minimum(jnp.sum((pad_ends[None, :] <= tiles[:, None]).astype(jnp.int32), axis=1), N_EXPERTS - 1)
    return tile_e, nused.reshape(1), rowtok, dest


def kernel(x, norm_mix_pre, norm_mix_post, norm_ffn_pre, norm_ffn_post, nsa_conv_w_in, cmp_pe, cmp_w1, cmp_b1, cmp_w2, cmp_b2, conv_w, conv_b, conv_ln_g, conv_ln_b, nsa_conv_w_out, ffn_w_gate, ffn_w_up, ffn_w_down, gmlp_w_in, gmlp_ln_g, gmlp_ln_b, gmlp_w_s, gmlp_b_s, gmlp_w_out, moe_w_router, moe_w_gate, moe_w_up, moe_w_down):
    b, s, d = x.shape
    n = b * s
    assert d == D_MODEL and s // SEL_LEN == SEL_LEN and s % TK_SEL == 0 and n % TM_PROJ == 0
    h0 = x.reshape(n, d)
    vec = lambda a: a.reshape(1, -1).astype(F32)

    wq, wkv, wg, wglu = _layer0_weights(nsa_conv_w_in[0])
    qpad, kv, gates, c = _proj0(h0, vec(norm_mix_pre[0]), wq, wkv, wg, wglu)
    kv3 = kv.reshape(b, s, 6 * N_KV * HEAD_DIM)

    nr = s // CMP_STRIDE
    kr = kv3[:, :, :2 * N_KV * HEAD_DIM].reshape(b, nr, CMP_STRIDE, 2, N_KV, HEAD_DIM)
    kr = kr.transpose(3, 0, 4, 1, 2, 5).reshape(2, b * N_KV, nr, CMP_STRIDE * HEAD_DIM)
    cmp = _compress(kr, cmp_pe[0].reshape(2, 1, CMP_LEN * HEAD_DIM).astype(F32), cmp_w1[0].astype(BF16),
                    cmp_b1[0].reshape(2, 1, CMP_HIDDEN).astype(F32), cmp_w2[0].astype(BF16),
                    cmp_b2[0].reshape(2, 1, HEAD_DIM).astype(F32))
    cmp = cmp.reshape(2, b, N_KV, nr, HEAD_DIM).transpose(0, 1, 3, 2, 4).reshape(2, b, nr, N_KV * HEAD_DIM)

    o_attn = _attention(qpad.reshape(b, s, N_HEADS * LANES), cmp[0], cmp[1], kv3,
                        gates.reshape(b, s, N_KV * LANES), _overlap_table(s), _expand_table(s), _place_table())
    cc = _conv_branch(c.reshape(b, s, CONV_CH), conv_w[0].reshape(CONV_WIDTH, CONV_CH).astype(F32),
                      vec(conv_b[0]), vec(conv_ln_g[0]), vec(conv_ln_b[0]))
    w_out = nsa_conv_w_out[0].astype(BF16)
    nw = N_HEADS * HEAD_DIM
    h1, u1 = _outproj(o_attn.reshape(n, nw), cc.reshape(n, CONV_CH), w_out[:nw], w_out[nw:], h0,
                      vec(norm_mix_post[0]), vec(norm_ffn_pre[0]))
    h2, u2 = _ffn(u1, ffn_w_gate[0].astype(BF16), ffn_w_up[0].astype(BF16), ffn_w_down[0].astype(BF16), h1,
                  vec(norm_ffn_post[0]), vec(norm_mix_pre[1]))

    bsb = jnp.repeat(gmlp_b_s[0].T.astype(F32), d // GMLP_GROUPS, axis=1)
    wr = jnp.pad(moe_w_router[0].astype(F32), ((0, 0), (0, LANES - N_EXPERTS)))
    wr_hi = wr.astype(BF16)
    wr_lo = (wr - wr_hi.astype(F32)).astype(BF16)
    h3, u4, route = _gmlp(u2, gmlp_w_in[0].astype(BF16), vec(gmlp_ln_g[0]), vec(gmlp_ln_b[0]),
                          gmlp_w_s[0].astype(F32), bsb, gmlp_w_out[0].astype(BF16), h2,
                          vec(norm_mix_post[1]), vec(norm_ffn_pre[1]), wr_hi, wr_lo)
    tile_e, nused, rowtok, dest = _dispatch_plan(route, n)
    y = _experts(tile_e, nused, rowtok, u4, moe_w_gate[0].astype(BF16), moe_w_up[0].astype(BF16),
                 moe_w_down[0].astype(BF16))
    h4 = _combine(dest, y, route, h3, vec(norm_ffn_post[1]))
    return h4.reshape(b, s, d)
```

```python
import functools

import numpy as np
import jax
import jax.numpy as jnp
from jax import lax
from jax.experimental import pallas as pl
from jax.experimental.pallas import tpu as pltpu

F32 = jnp.float32
BF16 = jnp.bfloat16

D_MODEL = 1024
N_HEADS = 8
N_KV = 2
HPG = N_HEADS // N_KV
HEAD_DIM = 64
CMP_LEN = 32
CMP_STRIDE = 16
CMP_HIDDEN = 256
SEL_LEN = 64
N_SEL = 16
WINDOW = 512
CONV_CH = D_MODEL // 2
CONV_WIDTH = 31
GMLP_GROUPS = 8
GMLP_CHUNK = 128
N_EXPERTS = 8
EPS = 1e-6
NEG = -1e30

LANES = 128
TQ = 128
TK_SEL = 256
TM_PROJ = 512
TS_CONV = 512
CONV_HALO = 32
TM_FFN = 512
TF_FFN = 1408
TM_MOE = 512
TF_MOE = 1792
TC_COMB = 256
VMEM_LIMIT = 56 * 1024 * 1024


def _params(sem):
    return pltpu.CompilerParams(dimension_semantics=sem, vmem_limit_bytes=VMEM_LIMIT)


def _dot(a, b):
    return jnp.dot(a, b, preferred_element_type=F32)


def _dot_nt(a, b):
    return lax.dot_general(a, b, (((1,), (1,)), ((), ())), preferred_element_type=F32)


def _rms(x, g):
    return x * lax.rsqrt(jnp.mean(x * x, axis=-1, keepdims=True) + EPS) * g


def _layer_norm(x, g, b):
    mu = jnp.mean(x, axis=-1, keepdims=True)
    xc = x - mu
    var = jnp.mean(xc * xc, axis=-1, keepdims=True)
    return xc * lax.rsqrt(var + EPS) * g + b


def _sigmoid(x):
    return 1.0 / (1.0 + jnp.exp(-x))


def _silu(x):
    return x * _sigmoid(x)


def _gelu_tanh(x):
    c = np.float32(np.sqrt(2.0 / np.pi))
    return 0.5 * x * (1.0 + jnp.tanh(c * (x + 0.044715 * (x * x * x))))


def _proj0_kernel(h_ref, g_ref, wqt_ref, wkv_ref, wvt_ref, wgt_ref, wglu_ref,
                  qt_ref, kv_ref, vt_ref, gt_ref, c_ref):
    u = _rms(h_ref[...], g_ref[...]).astype(BF16)
    qt_ref[0] = _dot_nt(wqt_ref[...], u).astype(BF16)
    kv_ref[...] = _dot(u, wkv_ref[...]).astype(BF16)
    vt_ref[0] = _dot_nt(wvt_ref[...], u).astype(BF16)
    gt_ref[0] = _sigmoid(_dot_nt(wgt_ref[...], u))
    glu = _dot(u, wglu_ref[...])
    c_ref[...] = glu[:, :CONV_CH] * _sigmoid(glu[:, CONV_CH:])


def _proj0(h, g, wqt, wkv, wvt, wgt, wglu, batch):
    n, d = h.shape
    tm = TM_PROJ
    seq = n // batch
    per = seq // tm
    full = lambda a: pl.BlockSpec(a.shape, lambda i: (0,) * a.ndim)
    row = lambda c: pl.BlockSpec((tm, c), lambda i: (i, 0))
    col = lambda r: pl.BlockSpec((1, r, tm), lambda i: (i // per, 0, i % per))
    return pl.pallas_call(
        _proj0_kernel,
        grid=(n // tm,),
        in_specs=[row(d), full(g), full(wqt), full(wkv), full(wvt), full(wgt), full(wglu)],
        out_specs=[col(wqt.shape[0]), row(wkv.shape[1]), col(wvt.shape[0]), col(wgt.shape[0]), row(CONV_CH)],
        out_shape=[jax.ShapeDtypeStruct((batch, wqt.shape[0], seq), BF16),
                   jax.ShapeDtypeStruct((n, wkv.shape[1]), BF16),
                   jax.ShapeDtypeStruct((batch, wvt.shape[0], seq), BF16),
                   jax.ShapeDtypeStruct((batch, wgt.shape[0], seq), F32),
                   jax.ShapeDtypeStruct((n, CONV_CH), F32)],
        compiler_params=_params(("parallel",)),
        name="proj0",
    )(h, g, wqt, wkv, wvt, wgt, wglu)


def _compress_kernel(kr_ref, pe_ref, w1_ref, b1_ref, w2_ref, b2_ref, o_ref):
    half = CMP_STRIDE * HEAD_DIM
    kr = kr_ref[0, 0].astype(F32)
    pe = pe_ref[0]
    top = (kr + pe[:, :half]).astype(BF16)
    bot = (kr + pe[:, half:]).astype(BF16)
    a = _dot(top, w1_ref[0, :half, :])
    b = _dot(bot, w1_ref[0, half:, :])
    nrow = b.shape[0]
    pre = a + pltpu.roll(b, nrow - 1, 0) + b1_ref[0]
    hid = _gelu_tanh(pre).astype(BF16)
    o_ref[0, 0] = (_dot(hid, w2_ref[0]) + b2_ref[0]).astype(o_ref.dtype)


def _compress(kr, pe, w1, b1, w2, b2):
    _, bg, nr, feat = kr.shape
    return pl.pallas_call(
        _compress_kernel,
        grid=(2, bg),
        in_specs=[pl.BlockSpec((1, 1, nr, feat), lambda j, i: (j, i, 0, 0)),
                  pl.BlockSpec((1, 1, 2 * feat), lambda j, i: (j, 0, 0)),
                  pl.BlockSpec((1, 2 * feat, CMP_HIDDEN), lambda j, i: (j, 0, 0)),
                  pl.BlockSpec((1, 1, CMP_HIDDEN), lambda j, i: (j, 0, 0)),
                  pl.BlockSpec((1, CMP_HIDDEN, HEAD_DIM), lambda j, i: (j, 0, 0)),
                  pl.BlockSpec((1, 1, HEAD_DIM), lambda j, i: (j, 0, 0))],
        out_specs=pl.BlockSpec((1, 1, nr, HEAD_DIM), lambda j, i: (j, i, 0, 0)),
        out_shape=jax.ShapeDtypeStruct((2, bg, nr, HEAD_DIM), BF16),
        compiler_params=_params(("parallel", "parallel")),
        name="compress",
    )(kr, pe, w1, b1, w2, b2)


def _col_reduce(x, op):
    rows = x.shape[0]
    part = op(x.reshape(4, rows // 4, x.shape[1]), axis=0)
    return op(part, axis=0, keepdims=True)


def _attn_kernel(qt_ref, kc_ref, vct_ref, ks_ref, vst_ref, kw_ref, vwt_ref, gt_ref,
                 ovt_ref, et_ref, o_ref, rhs_ref, m_ref, l_ref, acc_ref):
    qi = pl.program_id(1)
    q0 = qi * TQ
    nsb = SEL_LEN
    ncp = kc_ref.shape[1]
    wkeys = WINDOW + TQ
    groups = range(N_KV)
    tile4 = lambda a: jnp.concatenate([a] * HPG, axis=1)
    qgs = [jnp.concatenate(
        [qt_ref[0, LANES * (HPG * g + h):LANES * (HPG * g + h + 1), :] for h in range(HPG)], axis=1)
        for g in groups]

    n_idx = lax.broadcasted_iota(jnp.int32, (ncp, TQ), 0)
    t_idx = q0 + lax.broadcasted_iota(jnp.int32, (ncp, TQ), 1)
    cmask = tile4(n_idx * CMP_STRIDE + (CMP_LEN - 1) <= t_idx)
    o_c, imp_t = [], []
    for g in groups:
        s = jnp.where(cmask, _dot(kc_ref[0], qgs[g]), NEG)
        m = _col_reduce(s, jnp.max)
        p = jnp.where(cmask, jnp.exp2(s - m), 0.0)
        l = _col_reduce(p, jnp.sum)
        p = p / jnp.where(l > 0.0, l, 1.0)
        o_c.append(_dot(vct_ref[0], p.astype(BF16)))
        psum = p[:, 0:TQ] + p[:, TQ:2 * TQ] + p[:, 2 * TQ:3 * TQ] + p[:, 3 * TQ:4 * TQ]
        imp_t.append(_dot(ovt_ref[...], psum.astype(BF16))[:nsb])

    j_idx = lax.broadcasted_iota(jnp.int32, (nsb, TQ), 0)
    cur = (q0 + lax.broadcasted_iota(jnp.int32, (nsb, TQ), 1)) // SEL_LEN
    valid = j_idx <= cur
    forced = (j_idx == 0) | (j_idx == cur) | (j_idx == cur - 1)
    sub = lax.broadcasted_iota(jnp.int32, (8, TQ), 0)
    nch = nsb // 8
    for g in groups:
        score = jnp.where(valid, imp_t[g] + jnp.where(forced, 1e6, 0.0), -1e9)
        chunks = [score[8 * v:8 * v + 8] for v in range(nch)]
        counts = [jnp.zeros((8, TQ), F32) for _ in range(nch)]
        for i in range(nsb):
            row = jnp.broadcast_to(score[i:i + 1], (8, TQ))
            for v in range(nch):
                if i < 8 * v:
                    ahead = row >= chunks[v]
                elif i >= 8 * v + 8:
                    ahead = row > chunks[v]
                else:
                    ahead = jnp.where(sub > (i - 8 * v),
                                      jnp.where(row >= chunks[v], 1.0, 0.0),
                                      jnp.where(row > chunks[v], 1.0, 0.0)) > 0.5
                counts[v] = counts[v] + jnp.where(ahead, 1.0, 0.0)
        bias_t = jnp.concatenate(
            [jnp.where((c < float(N_SEL)) & valid[8 * v:8 * v + 8], 0.0, NEG) for v, c in enumerate(counts)]
            + [jnp.zeros((LANES - nsb, TQ), F32)], axis=0).astype(BF16)
        rhs_ref[g] = jnp.concatenate([qgs[g], tile4(bias_t)], axis=0)

    m_ref[...] = jnp.full(m_ref.shape, NEG, F32)
    l_ref[...] = jnp.zeros(l_ref.shape, F32)
    acc_ref[...] = jnp.zeros(acc_ref.shape, F32)

    def sel_tile(kt, diagonal):
        k0 = pl.multiple_of(kt * TK_SEL, TK_SEL)
        lhs = jnp.concatenate([ks_ref[0, pl.ds(k0, TK_SEL), :], et_ref[pl.ds(k0, TK_SEL), :]], axis=1)
        vt = vst_ref[0, :, pl.ds(k0, TK_SEL)]
        if diagonal:
            kpos = k0 + lax.broadcasted_iota(jnp.int32, (TK_SEL, TQ), 0)
            tpos = q0 + lax.broadcasted_iota(jnp.int32, (TK_SEL, TQ), 1)
            causal = tile4(kpos <= tpos)
        for g in groups:
            sc = _dot(lhs, rhs_ref[g])
            if diagonal:
                sc = jnp.where(causal, sc, NEG)
            m_i = m_ref[g]
            m_new = jnp.maximum(m_i, _col_reduce(sc, jnp.max))
            alpha = jnp.exp2(m_i - m_new)
            pexp = jnp.exp2(sc - m_new)
            l_ref[g] = alpha * l_ref[g] + _col_reduce(pexp, jnp.sum)
            acc_ref[g] = alpha * acc_ref[g] + _dot(vt, pexp.astype(BF16))
            m_ref[g] = m_new

    last = q0 // TK_SEL

    def sel_body(kt, carry):
        sel_tile(kt, False)
        return carry

    lax.fori_loop(0, last, sel_body, 0)
    sel_tile(last, True)

    k0 = pl.multiple_of(jnp.maximum(q0 - WINDOW, 0), TQ)
    kpos = k0 + lax.broadcasted_iota(jnp.int32, (wkeys, TQ), 0)
    tpos = q0 + lax.broadcasted_iota(jnp.int32, (wkeys, TQ), 1)
    wmask = tile4((kpos <= tpos) & (kpos > tpos - WINDOW))
    kwin = kw_ref[0, pl.ds(k0, wkeys), :]
    vwin = vwt_ref[0, :, pl.ds(k0, wkeys)]
    o_w = []
    for g in groups:
        sc = jnp.where(wmask, _dot(kwin, qgs[g]), NEG)
        pexp = jnp.exp2(sc - _col_reduce(sc, jnp.max))
        o_w.append(_dot(vwin, pexp.astype(BF16)) / _col_reduce(pexp, jnp.sum))

    blocks = []
    for g in groups:
        o_s = acc_ref[g] / l_ref[g]
        for h in range(HPG):
            r0 = LANES * g + 3 * h
            cols = slice(h * TQ, (h + 1) * TQ)
            og = (gt_ref[0, r0:r0 + 1, :] * o_c[g][:, cols] + gt_ref[0, r0 + 1:r0 + 2, :] * o_s[:, cols]
                  + gt_ref[0, r0 + 2:r0 + 3, :] * o_w[g][:, cols])
            blocks.append(og[HEAD_DIM * g:HEAD_DIM * (g + 1)])
    o_ref[0] = jnp.concatenate(blocks, axis=0).T.astype(o_ref.dtype)


def _attention(qt, kc, vct, kv, vt, gt, ovt, et):
    b, _, s = qt.shape
    ncp = kc.shape[1]
    kspec = lambda j: pl.BlockSpec((1, s, LANES), lambda bi, qi: (bi, 0, j))
    vspec = lambda j: pl.BlockSpec((1, LANES, s), lambda bi, qi: (bi, j, 0))
    return pl.pallas_call(
        _attn_kernel,
        grid=(b, s // TQ),
        in_specs=[pl.BlockSpec((1, N_HEADS * LANES, TQ), lambda bi, qi: (bi, 0, qi)),
                  pl.BlockSpec((1, ncp, LANES), lambda bi, qi: (bi, 0, 0)),
                  pl.BlockSpec((1, LANES, ncp), lambda bi, qi: (bi, 0, 0)),
                  kspec(2), vspec(0), kspec(4), vspec(1),
                  pl.BlockSpec((1, N_KV * LANES, TQ), lambda bi, qi: (bi, 0, qi)),
                  pl.BlockSpec(ovt.shape, lambda bi, qi: (0, 0)),
                  pl.BlockSpec(et.shape, lambda bi, qi: (0, 0))],
        out_specs=pl.BlockSpec((1, TQ, N_HEADS * HEAD_DIM), lambda bi, qi: (bi, qi, 0)),
        out_shape=jax.ShapeDtypeStruct((b, s, N_HEADS * HEAD_DIM), BF16),
        scratch_shapes=[pltpu.VMEM((N_KV, 2 * LANES, HPG * TQ), BF16),
                        pltpu.VMEM((N_KV, 1, HPG * TQ), F32), pltpu.VMEM((N_KV, 1, HPG * TQ), F32),
                        pltpu.VMEM((N_KV, LANES, HPG * TQ), F32)],
        compiler_params=_params(("parallel", "arbitrary")),
        name="nsa_attention",
    )(qt, kc, vct, kv, vt, kv, vt, gt, ovt, et)


def _conv_kernel(c_ref, halo_ref, cw_ref, cb_ref, g_ref, b_ref, o_ref, xs_ref):
    i = pl.program_id(1)
    ts = c_ref.shape[1]
    xs_ref[0:CONV_HALO, :] = jnp.where(i > 0, halo_ref[0], 0.0)
    xs_ref[CONV_HALO:CONV_HALO + ts, :] = c_ref[0]
    acc = jnp.zeros((ts, CONV_CH), F32)
    for w in range(CONV_WIDTH):
        off = CONV_HALO - (CONV_WIDTH - 1) + w
        acc = acc + xs_ref[off:off + ts, :] * cw_ref[w:w + 1, :]
    y = _layer_norm(acc + cb_ref[...], g_ref[...], b_ref[...])
    o_ref[0] = _silu(y).astype(o_ref.dtype)


def _conv_branch(c, cw, cb, ln_g, ln_b):
    b, s, ch = c.shape
    ts = TS_CONV
    per = ts // CONV_HALO
    vec = pl.BlockSpec((1, ch), lambda bi, i: (0, 0))
    return pl.pallas_call(
        _conv_kernel,
        grid=(b, s // ts),
        in_specs=[pl.BlockSpec((1, ts, ch), lambda bi, i: (bi, i, 0)),
                  pl.BlockSpec((1, CONV_HALO, ch), lambda bi, i: (bi, jnp.maximum(i * per - 1, 0), 0)),
                  pl.BlockSpec((CONV_WIDTH, ch), lambda bi, i: (0, 0)), vec, vec, vec],
        out_specs=pl.BlockSpec((1, ts, ch), lambda bi, i: (bi, i, 0)),
        out_shape=jax.ShapeDtypeStruct((b, s, ch), BF16),
        scratch_shapes=[pltpu.VMEM((CONV_HALO + ts, ch), F32)],
        compiler_params=_params(("parallel", "arbitrary")),
        name="conv_branch",
    )(c, c, cw, cb, ln_g, ln_b)


def _outproj_kernel(oa_ref, cc_ref, wt_ref, wb_ref, h_ref, gpost_ref, gnext_ref, h1_ref, u_ref):
    m = _dot(oa_ref[...], wt_ref[...]) + _dot(cc_ref[...], wb_ref[...])
    h1 = h_ref[...] + _rms(m, gpost_ref[...])
    h1_ref[...] = h1
    u_ref[...] = _rms(h1, gnext_ref[...]).astype(u_ref.dtype)


def _outproj(oa, cc, wt, wb, h, gpost, gnext):
    n, d = h.shape
    tm = TM_PROJ
    full = lambda a: pl.BlockSpec(a.shape, lambda i: (0,) * a.ndim)
    row = lambda c: pl.BlockSpec((tm, c), lambda i: (i, 0))
    return pl.pallas_call(
        _outproj_kernel,
        grid=(n // tm,),
        in_specs=[row(oa.shape[1]), row(cc.shape[1]), full(wt), full(wb), row(d), full(gpost), full(gnext)],
        out_specs=[row(d), row(d)],
        out_shape=[jax.ShapeDtypeStruct((n, d), F32), jax.ShapeDtypeStruct((n, d), BF16)],
        compiler_params=_params(("parallel",)),
        name="outproj",
    )(oa, cc, wt, wb, h, gpost, gnext)


def _ffn_kernel(u_ref, wg_ref, wu_ref, wd_ref, h_ref, gpost_ref, gnext_ref, h2_ref, u2_ref, acc_ref):
    f = pl.program_id(1)

    @pl.when(f == 0)
    def _():
        acc_ref[...] = jnp.zeros_like(acc_ref)

    u = u_ref[...]
    act = (_silu(_dot(u, wg_ref[...])) * _dot(u, wu_ref[...])).astype(BF16)
    acc_ref[...] += _dot(act, wd_ref[...])

    @pl.when(f == pl.num_programs(1) - 1)
    def _():
        h2 = h_ref[...] + _rms(acc_ref[...], gpost_ref[...])
        h2_ref[...] = h2
        u2_ref[...] = _rms(h2, gnext_ref[...]).astype(u2_ref.dtype)


def _ffn(u, wg, wu, wd, h, gpost, gnext):
    n, d = h.shape
    ff = wg.shape[1]
    tm, tf = TM_FFN, TF_FFN
    row = pl.BlockSpec((tm, d), lambda i, f: (i, 0))
    vec = pl.BlockSpec((1, d), lambda i, f: (0, 0))
    return pl.pallas_call(
        _ffn_kernel,
        grid=(n // tm, ff // tf),
        in_specs=[row, pl.BlockSpec((d, tf), lambda i, f: (0, f)), pl.BlockSpec((d, tf), lambda i, f: (0, f)),
                  pl.BlockSpec((tf, d), lambda i, f: (f, 0)), row, vec, vec],
        out_specs=[row, row],
        out_shape=[jax.ShapeDtypeStruct((n, d), F32), jax.ShapeDtypeStruct((n, d), BF16)],
        scratch_shapes=[pltpu.VMEM((tm, d), F32)],
        compiler_params=_params(("parallel", "arbitrary")),
        name="dense_ffn",
    )(u, wg, wu, wd, h, gpost, gnext)


def _gmlp_kernel(u_ref, win_ref, lng_ref, lnb_ref, ws_ref, bs_ref, wout_ref, h_ref, gpost_ref, gnext_ref,
                 wrh_ref, wrl_ref, h3_ref, u4_ref, route_ref, mix_ref):
    tm = u_ref.shape[0]
    width = wout_ref.shape[0]
    z = _gelu_tanh(_dot(u_ref[...], win_ref[...]))
    z1 = z[:, :width]
    z2 = _layer_norm(z[:, width:], lng_ref[...], lnb_ref[...]).astype(BF16)
    r_idx = lax.broadcasted_iota(jnp.int32, (GMLP_CHUNK, GMLP_CHUNK), 0)
    c_idx = lax.broadcasted_iota(jnp.int32, (GMLP_CHUNK, GMLP_CHUNK), 1)
    gw = width // GMLP_GROUPS
    for g in range(GMLP_GROUPS):
        wsg = jnp.where(c_idx <= r_idx, ws_ref[g], 0.0).astype(BF16)
        for c in range(tm // GMLP_CHUNK):
            rows = slice(c * GMLP_CHUNK, (c + 1) * GMLP_CHUNK)
            cols = slice(g * gw, (g + 1) * gw)
            mix_ref[rows, cols] = _dot(wsg, z2[rows, cols]) + bs_ref[:, cols]
    gated = (z1 * mix_ref[...]).astype(BF16)
    y = _dot(gated, wout_ref[...])
    h3 = h_ref[...] + _rms(y, gpost_ref[...])
    h3_ref[...] = h3
    u4 = _rms(h3, gnext_ref[...])
    u4_ref[...] = u4

    u_hi = u4.astype(BF16)
    u_lo = (u4 - u_hi.astype(F32)).astype(BF16)
    logits = _dot(u_hi, wrh_ref[...]) + (_dot(u_lo, wrh_ref[...]) + _dot(u_hi, wrl_ref[...]))
    lane = lax.broadcasted_iota(jnp.int32, (tm, LANES), 1).astype(F32)
    lg = jnp.where(lane < float(N_EXPERTS), logits, NEG)
    m1 = jnp.max(lg, axis=-1, keepdims=True)
    i1 = jnp.min(jnp.where(lg == m1, lane, float(LANES)), axis=-1, keepdims=True)
    lg2 = jnp.where(lane == i1, NEG, lg)
    m2 = jnp.max(lg2, axis=-1, keepdims=True)
    i2 = jnp.min(jnp.where(lg2 == m2, lane, float(LANES)), axis=-1, keepdims=True)
    e2 = jnp.exp(m2 - m1)
    den = 1.0 + e2
    route_ref[...] = jnp.where(lane == 0.0, i1, jnp.where(lane == 1.0, i2,
                               jnp.where(lane == 2.0, 1.0 / den, jnp.where(lane == 3.0, e2 / den, 0.0))))


def _gmlp(u, win, lng, lnb, ws, bsb, wout, h, gpost, gnext, wrh, wrl):
    n, d = h.shape
    tm = TM_PROJ
    full = lambda a: pl.BlockSpec(a.shape, lambda i: (0,) * a.ndim)
    row = lambda c: pl.BlockSpec((tm, c), lambda i: (i, 0))
    return pl.pallas_call(
        _gmlp_kernel,
        grid=(n // tm,),
        in_specs=[row(d), full(win), full(lng), full(lnb), full(ws), full(bsb), full(wout), row(d),
                  full(gpost), full(gnext), full(wrh), full(wrl)],
        out_specs=[row(d), row(d), row(LANES)],
        out_shape=[jax.ShapeDtypeStruct((n, d), F32), jax.ShapeDtypeStruct((n, d), F32),
                   jax.ShapeDtypeStruct((n, LANES), F32)],
        scratch_shapes=[pltpu.VMEM((tm, wout.shape[0]), F32)],
        compiler_params=_params(("parallel",)),
        name="gmlp_router",
    )(u, win, lng, lnb, ws, bsb, wout, h, gpost, gnext, wrh, wrl)


def _expert_kernel(tile_e_ref, nused_ref, rowtok_ref, x_hbm, wg_ref, wu_ref, wd_ref, y_ref,
                   xg_ref, xb_ref, acc_ref, sem):
    i = pl.program_id(0)
    f = pl.program_id(1)
    tm = xg_ref.shape[0]
    active = i < nused_ref[0]

    def row_copy(tok, r):
        return pltpu.make_async_copy(x_hbm.at[pl.ds(tok, 1)], xg_ref.at[pl.ds(r, 1)], sem)

    @pl.when(active & (f == 0))
    def _():
        def issue(r, carry):
            row_copy(rowtok_ref[i * tm + r], r).start()
            return carry

        lax.fori_loop(0, tm, issue, 0)

        def wait(r, carry):
            row_copy(0, r).wait()
            return carry

        lax.fori_loop(0, tm, wait, 0)
        xb_ref[...] = xg_ref[...].astype(BF16)
        acc_ref[...] = jnp.zeros_like(acc_ref)

    @pl.when(active)
    def _():
        x = xb_ref[...]
        act = (_silu(_dot(x, wg_ref[0])) * _dot(x, wu_ref[0])).astype(BF16)
        acc_ref[...] += _dot(act, wd_ref[0])

    last = f == pl.num_programs(1) - 1

    @pl.when(active & last)
    def _():
        y_ref[...] = acc_ref[...]

    @pl.when(jnp.logical_not(active) & last)
    def _():
        y_ref[...] = jnp.zeros_like(y_ref)


def _experts(tile_e, nused, rowtok, x, wg, wu, wd):
    n_rows = rowtok.shape[0]
    d = x.shape[1]
    ff = wg.shape[2]
    tm, tf = TM_MOE, TF_MOE
    nf = ff // tf

    def fidx(i, f, te, nu):
        return jnp.where(i < nu[0], f, nf - 1)

    grid_spec = pltpu.PrefetchScalarGridSpec(
        num_scalar_prefetch=3,
        grid=(n_rows // tm, nf),
        in_specs=[pl.BlockSpec(memory_space=pl.ANY),
                  pl.BlockSpec((1, d, tf), lambda i, f, te, nu, rt: (te[i], 0, fidx(i, f, te, nu))),
                  pl.BlockSpec((1, d, tf), lambda i, f, te, nu, rt: (te[i], 0, fidx(i, f, te, nu))),
                  pl.BlockSpec((1, tf, d), lambda i, f, te, nu, rt: (te[i], fidx(i, f, te, nu), 0))],
        out_specs=pl.BlockSpec((tm, d), lambda i, f, te, nu, rt: (i, 0)),
        scratch_shapes=[pltpu.VMEM((tm, d), F32), pltpu.VMEM((tm, d), BF16), pltpu.VMEM((tm, d), F32),
                        pltpu.SemaphoreType.DMA],
    )
    return pl.pallas_call(
        _expert_kernel,
        grid_spec=grid_spec,
        out_shape=jax.ShapeDtypeStruct((n_rows, d), F32),
        compiler_params=_params(("arbitrary", "arbitrary")),
        name="moe_experts",
    )(tile_e, nused, rowtok, x, wg, wu, wd)


def _combine_kernel(dest_ref, y_hbm, route_ref, h_ref, gpost_ref, o_ref, ya_ref, yb_ref, sem):
    i = pl.program_id(0)
    tc = ya_ref.shape[0]

    def copies(r):
        p = 2 * (i * tc + r)
        return (pltpu.make_async_copy(y_hbm.at[pl.ds(dest_ref[p], 1)], ya_ref.at[pl.ds(r, 1)], sem.at[0]),
                pltpu.make_async_copy(y_hbm.at[pl.ds(dest_ref[p + 1], 1)], yb_ref.at[pl.ds(r, 1)], sem.at[1]))

    def issue(r, carry):
        ca, cb = copies(r)
        ca.start()
        cb.start()
        return carry

    lax.fori_loop(0, tc, issue, 0)

    def wait(r, carry):
        ca, cb = copies(r)
        ca.wait()
        cb.wait()
        return carry

    lax.fori_loop(0, tc, wait, 0)
    route = route_ref[...]
    moe = route[:, 2:3] * ya_ref[...] + route[:, 3:4] * yb_ref[...]
    o_ref[...] = h_ref[...] + _rms(moe, gpost_ref[...])


def _combine(dest, y, route, h, gpost):
    n, d = h.shape
    tc = TC_COMB
    grid_spec = pltpu.PrefetchScalarGridSpec(
        num_scalar_prefetch=1,
        grid=(n // tc,),
        in_specs=[pl.BlockSpec(memory_space=pl.ANY),
                  pl.BlockSpec((tc, LANES), lambda i, ds: (i, 0)),
                  pl.BlockSpec((tc, d), lambda i, ds: (i, 0)),
                  pl.BlockSpec((1, d), lambda i, ds: (0, 0))],
        out_specs=pl.BlockSpec((tc, d), lambda i, ds: (i, 0)),
        scratch_shapes=[pltpu.VMEM((tc, d), F32), pltpu.VMEM((tc, d), F32), pltpu.SemaphoreType.DMA((2,))],
    )
    return pl.pallas_call(
        _combine_kernel,
        grid_spec=grid_spec,
        out_shape=jax.ShapeDtypeStruct((n, d), F32),
        compiler_params=_params(("arbitrary",)),
        name="moe_combine",
    )(dest, y, route, h, gpost)


def _overlap_table_t(seq):
    nc = (seq - CMP_LEN) // CMP_STRIDE + 1
    ncp = seq // CMP_STRIDE
    nsb = seq // SEL_LEN
    cs = np.arange(ncp)[None, :] * CMP_STRIDE
    js = np.arange(LANES)[:, None] * SEL_LEN
    ov = (cs < js + SEL_LEN) & (cs + CMP_LEN > js) & (np.arange(ncp)[None, :] < nc) & (np.arange(LANES)[:, None] < nsb)
    return jnp.asarray(ov.astype(np.float32), BF16)


def _expand_table_t(seq):
    e = (np.arange(seq)[:, None] // SEL_LEN) == np.arange(LANES)[None, :]
    return jnp.asarray(e.astype(np.float32), BF16)


def _layer0_weights(w_in):
    d = w_in.shape[0]
    qc = N_HEADS * HEAD_DIM
    kvc = 6 * N_KV * HEAD_DIM
    gc = 3 * N_HEADS
    gh = N_KV * HEAD_DIM
    wq = (w_in[:, :qc] * (HEAD_DIM ** -0.5 * np.log2(np.e))).reshape(d, N_KV, HPG, HEAD_DIM)
    zeros = jnp.zeros_like(wq[:, 0])
    wq_pad = jnp.stack([jnp.concatenate([wq[:, 0], zeros], axis=-1),
                        jnp.concatenate([zeros, wq[:, 1]], axis=-1)], axis=1)
    wqt = wq_pad.reshape(d, N_HEADS * LANES).T.astype(BF16)
    wkv = w_in[:, qc:qc + kvc]
    wvt = jnp.concatenate([wkv[:, 3 * gh:4 * gh], wkv[:, 5 * gh:6 * gh]], axis=1).T.astype(BF16)
    wg = w_in[:, qc + kvc:qc + kvc + gc].reshape(d, N_KV, 3 * HPG)
    wgt = jnp.pad(wg, ((0, 0), (0, 0), (0, LANES - 3 * HPG))).reshape(d, N_KV * LANES).T.astype(BF16)
    wglu = w_in[:, qc + kvc + gc:].astype(BF16)
    return wqt, wkv.astype(BF16), wvt, wgt, wglu


def _dispatch_plan(route, n_tok):
    tm = TM_MOE
    e_flat = route[:, 0:2].astype(jnp.int32).reshape(-1)
    onehot = (e_flat[:, None] == jnp.arange(N_EXPERTS, dtype=jnp.int32)[None, :]).astype(jnp.int32)
    rank = jnp.sum((jnp.cumsum(onehot, axis=0) - onehot) * onehot, axis=1)
    counts = jnp.sum(onehot, axis=0)
    padded = (counts + tm - 1) // tm * tm
    pad_ends = jnp.cumsum(padded)
    pad_starts = pad_ends - padded
    dest = (pad_starts[e_flat] + rank).astype(jnp.int32)
    n_rows = 2 * n_tok + N_EXPERTS * tm
    n_tiles = n_rows // tm
    rowtok = jnp.zeros((n_rows,), jnp.int32).at[dest].set(jnp.arange(2 * n_tok, dtype=jnp.int32) // 2)
    nused = (pad_ends[-1] // tm).astype(jnp.int32)
    tiles = jnp.minimum(jnp.arange(n_tiles, dtype=jnp.int32), nused - 1) * tm
    tile_e = jnp.minimum(jnp.sum((pad_ends[None, :] <= tiles[:, None]).astype(jnp.int32), axis=1), N_EXPERTS - 1)
    return tile_e, nused.reshape(1), rowtok, dest


def kernel(x, norm_mix_pre, norm_mix_post, norm_ffn_pre, norm_ffn_post, nsa_conv_w_in, cmp_pe, cmp_w1, cmp_b1, cmp_w2, cmp_b2, conv_w, conv_b, conv_ln_g, conv_ln_b, nsa_conv_w_out, ffn_w_gate, ffn_w_up, ffn_w_down, gmlp_w_in, gmlp_ln_g, gmlp_ln_b, gmlp_w_s, gmlp_b_s, gmlp_w_out, moe_w_router, moe_w_gate, moe_w_up, moe_w_down):
    b, s, d = x.shape
    n = b * s
    assert d == D_MODEL and s // SEL_LEN == SEL_LEN and s % (2 * TK_SEL) == 0 and n % TM_PROJ == 0
    h0 = x.reshape(n, d)
    vec = lambda a: a.reshape(1, -1).astype(F32)

    wqt, wkv, wvt, wgt, wglu = _layer0_weights(nsa_conv_w_in[0])
    qt, kv, vt, gt, c = _proj0(h0, vec(norm_mix_pre[0]), wqt, wkv, wvt, wgt, wglu, b)
    kv3 = kv.reshape(b, s, 6 * N_KV * HEAD_DIM)

    nr = s // CMP_STRIDE
    kr = kv3[:, :, :2 * N_KV * HEAD_DIM].reshape(b, nr, CMP_STRIDE, 2, N_KV, HEAD_DIM)
    kr = kr.transpose(3, 0, 4, 1, 2, 5).reshape(2, b * N_KV, nr, CMP_STRIDE * HEAD_DIM)
    cmp = _compress(kr, cmp_pe[0].reshape(2, 1, CMP_LEN * HEAD_DIM).astype(F32), cmp_w1[0].astype(BF16),
                    cmp_b1[0].reshape(2, 1, CMP_HIDDEN).astype(F32), cmp_w2[0].astype(BF16),
                    cmp_b2[0].reshape(2, 1, HEAD_DIM).astype(F32))
    cmp = cmp.reshape(2, b, N_KV, nr, HEAD_DIM)
    kc = cmp[0].transpose(0, 2, 1, 3).reshape(b, nr, N_KV * HEAD_DIM)
    vct = cmp[1].transpose(0, 1, 3, 2).reshape(b, N_KV * HEAD_DIM, nr)

    o_attn = _attention(qt, kc, vct, kv3, vt, gt, _overlap_table_t(s), _expand_table_t(s))
    cc = _conv_branch(c.reshape(b, s, CONV_CH), conv_w[0].reshape(CONV_WIDTH, CONV_CH).astype(F32),
                      vec(conv_b[0]), vec(conv_ln_g[0]), vec(conv_ln_b[0]))
    w_out = nsa_conv_w_out[0].astype(BF16)
    nw = N_HEADS * HEAD_DIM
    h1, u1 = _outproj(o_attn.reshape(n, nw), cc.reshape(n, CONV_CH), w_out[:nw], w_out[nw:], h0,
                      vec(norm_mix_post[0]), vec(norm_ffn_pre[0]))
    h2, u2 = _ffn(u1, ffn_w_gate[0].astype(BF16), ffn_w_up[0].astype(BF16), ffn_w_down[0].astype(BF16), h1,
                  vec(norm_ffn_post[0]), vec(norm_mix_pre[1]))

    bsb = jnp.repeat(gmlp_b_s[0].T.astype(F32), d // GMLP_GROUPS, axis=1)
    wr = jnp.pad(moe_w_router[0].astype(F32), ((0, 0), (0, LANES - N_EXPERTS)))
    wr_hi = wr.astype(BF16)
    wr_lo = (wr - wr_hi.astype(F32)).astype(BF16)
    h3, u4, route = _gmlp(u2, gmlp_w_in[0].astype(BF16), vec(gmlp_ln_g[0]), vec(gmlp_ln_b[0]),
                          gmlp_w_s[0].astype(F32), bsb, gmlp_w_out[0].astype(BF16), h2,
                          vec(norm_mix_post[1]), vec(norm_ffn_pre[1]), wr_hi, wr_lo)
    tile_e, nused, rowtok, dest = _dispatch_plan(route, n)
    y = _experts(tile_e, nused, rowtok, u4, moe_w_gate[0].astype(BF16), moe_w_up[0].astype(BF16),
                 moe_w_down[0].astype(BF16))
    h4 = _combine(dest, y, route, h3, vec(norm_ffn_post[1]))
    return h4.reshape(b, s, d)
```

```python
import functools

import numpy as np
import jax
import jax.numpy as jnp
from jax import lax
from jax.experimental import pallas as pl
from jax.experimental.pallas import tpu as pltpu

F32 = jnp.float32
BF16 = jnp.bfloat16

D_MODEL = 1024
N_HEADS = 8
N_KV = 2
HPG = N_HEADS // N_KV
HEAD_DIM = 64
CMP_LEN = 32
CMP_STRIDE = 16
CMP_HIDDEN = 256
SEL_LEN = 64
N_SEL = 16
WINDOW = 512
CONV_CH = D_MODEL // 2
CONV_WIDTH = 31
GMLP_GROUPS = 8
GMLP_CHUNK = 128
N_EXPERTS = 8
EPS = 1e-6
NEG = -1e30

LANES = 128
TQ = 128
TK_SEL = 256
TM_PROJ = 512
TS_CONV = 512
CONV_HALO = 32
TM_FFN = 512
TF_FFN = 1408
TM_MOE = 512
TF_MOE = 1792
TC_COMB = 256
VMEM_LIMIT = 56 * 1024 * 1024


def _params(sem):
    return pltpu.CompilerParams(dimension_semantics=sem, vmem_limit_bytes=VMEM_LIMIT)


def _dot(a, b):
    return jnp.dot(a, b, preferred_element_type=F32)


def _dot_nt(a, b):
    return lax.dot_general(a, b, (((1,), (1,)), ((), ())), preferred_element_type=F32)


def _rms(x, g):
    return x * lax.rsqrt(jnp.mean(x * x, axis=-1, keepdims=True) + EPS) * g


def _layer_norm(x, g, b):
    mu = jnp.mean(x, axis=-1, keepdims=True)
    xc = x - mu
    var = jnp.mean(xc * xc, axis=-1, keepdims=True)
    return xc * lax.rsqrt(var + EPS) * g + b


def _sigmoid(x):
    return 1.0 / (1.0 + jnp.exp(-x))


def _silu(x):
    return x * _sigmoid(x)


def _gelu_tanh(x):
    c = np.float32(np.sqrt(2.0 / np.pi))
    return 0.5 * x * (1.0 + jnp.tanh(c * (x + 0.044715 * (x * x * x))))


def _proj0_kernel(h_ref, g_ref, wqt_ref, wkv_ref, wvt_ref, wgt_ref, wglu_ref,
                  qt_ref, kv_ref, vt_ref, gt_ref, c_ref):
    u = _rms(h_ref[...], g_ref[...]).astype(BF16)
    qt_ref[0] = _dot_nt(wqt_ref[...], u).astype(BF16)
    kv_ref[...] = _dot(u, wkv_ref[...]).astype(BF16)
    vt_ref[0] = _dot_nt(wvt_ref[...], u).astype(BF16)
    gt_ref[0] = _sigmoid(_dot_nt(wgt_ref[...], u))
    glu = _dot(u, wglu_ref[...])
    c_ref[...] = glu[:, :CONV_CH] * _sigmoid(glu[:, CONV_CH:])


def _proj0(h, g, wqt, wkv, wvt, wgt, wglu, batch):
    n, d = h.shape
    tm = TM_PROJ
    seq = n // batch
    per = seq // tm
    full = lambda a: pl.BlockSpec(a.shape, lambda i: (0,) * a.ndim)
    row = lambda c: pl.BlockSpec((tm, c), lambda i: (i, 0))
    col = lambda r: pl.BlockSpec((1, r, tm), lambda i: (i // per, 0, i % per))
    return pl.pallas_call(
        _proj0_kernel,
        grid=(n // tm,),
        in_specs=[row(d), full(g), full(wqt), full(wkv), full(wvt), full(wgt), full(wglu)],
        out_specs=[col(wqt.shape[0]), row(wkv.shape[1]), col(wvt.shape[0]), col(wgt.shape[0]), row(CONV_CH)],
        out_shape=[jax.ShapeDtypeStruct((batch, wqt.shape[0], seq), BF16),
                   jax.ShapeDtypeStruct((n, wkv.shape[1]), BF16),
                   jax.ShapeDtypeStruct((batch, wvt.shape[0], seq), BF16),
                   jax.ShapeDtypeStruct((batch, wgt.shape[0], seq), F32),
                   jax.ShapeDtypeStruct((n, CONV_CH), F32)],
        compiler_params=_params(("parallel",)),
        name="proj0",
    )(h, g, wqt, wkv, wvt, wgt, wglu)


def _compress_kernel(kr_ref, pe_ref, w1_ref, b1_ref, w2_ref, b2_ref, o_ref):
    half = CMP_STRIDE * HEAD_DIM
    kr = kr_ref[0, 0].astype(F32)
    pe = pe_ref[0]
    top = (kr + pe[:, :half]).astype(BF16)
    bot = (kr + pe[:, half:]).astype(BF16)
    a = _dot(top, w1_ref[0, :half, :])
    b = _dot(bot, w1_ref[0, half:, :])
    nrow = b.shape[0]
    pre = a + pltpu.roll(b, nrow - 1, 0) + b1_ref[0]
    hid = _gelu_tanh(pre).astype(BF16)
    o_ref[0, 0] = (_dot(hid, w2_ref[0]) + b2_ref[0]).astype(o_ref.dtype)


def _compress(kr, pe, w1, b1, w2, b2):
    _, bg, nr, feat = kr.shape
    return pl.pallas_call(
        _compress_kernel,
        grid=(2, bg),
        in_specs=[pl.BlockSpec((1, 1, nr, feat), lambda j, i: (j, i, 0, 0)),
                  pl.BlockSpec((1, 1, 2 * feat), lambda j, i: (j, 0, 0)),
                  pl.BlockSpec((1, 2 * feat, CMP_HIDDEN), lambda j, i: (j, 0, 0)),
                  pl.BlockSpec((1, 1, CMP_HIDDEN), lambda j, i: (j, 0, 0)),
                  pl.BlockSpec((1, CMP_HIDDEN, HEAD_DIM), lambda j, i: (j, 0, 0)),
                  pl.BlockSpec((1, 1, HEAD_DIM), lambda j, i: (j, 0, 0))],
        out_specs=pl.BlockSpec((1, 1, nr, HEAD_DIM), lambda j, i: (j, i, 0, 0)),
        out_shape=jax.ShapeDtypeStruct((2, bg, nr, HEAD_DIM), BF16),
        compiler_params=_params(("parallel", "parallel")),
        name="compress",
    )(kr, pe, w1, b1, w2, b2)


def _col_reduce(x, op):
    rows = x.shape[0]
    part = op(x.reshape(4, rows // 4, x.shape[1]), axis=0)
    return op(part, axis=0, keepdims=True)


def _attn_kernel(qt_ref, kc_ref, vct_ref, ks_ref, vst_ref, kw_ref, vwt_ref, gt_ref,
                 ovt_ref, et_ref, o_ref, rhs_ref, m_ref, l_ref, acc_ref):
    qi = pl.program_id(1)
    q0 = qi * TQ
    nsb = SEL_LEN
    ncp = kc_ref.shape[1]
    wkeys = WINDOW + TQ
    groups = range(N_KV)
    tile4 = lambda a: jnp.concatenate([a] * HPG, axis=1)
    qgs = [jnp.concatenate(
        [qt_ref[0, LANES * (HPG * g + h):LANES * (HPG * g + h + 1), :] for h in range(HPG)], axis=1)
        for g in groups]

    n_idx = lax.broadcasted_iota(jnp.int32, (ncp, TQ), 0)
    t_idx = q0 + lax.broadcasted_iota(jnp.int32, (ncp, TQ), 1)
    cmask = tile4(n_idx * CMP_STRIDE + (CMP_LEN - 1) <= t_idx)
    o_c, imp_t = [], []
    for g in groups:
        s = jnp.where(cmask, _dot(kc_ref[0], qgs[g]), NEG)
        m = _col_reduce(s, jnp.max)
        p = jnp.where(cmask, jnp.exp2(s - m), 0.0)
        l = _col_reduce(p, jnp.sum)
        p = p / jnp.where(l > 0.0, l, 1.0)
        o_c.append(_dot(vct_ref[0], p.astype(BF16)))
        psum = p[:, 0:TQ] + p[:, TQ:2 * TQ] + p[:, 2 * TQ:3 * TQ] + p[:, 3 * TQ:4 * TQ]
        imp_t.append(_dot(ovt_ref[...], psum.astype(BF16))[:nsb])

    j_idx = lax.broadcasted_iota(jnp.int32, (nsb, TQ), 0)
    cur = (q0 + lax.broadcasted_iota(jnp.int32, (nsb, TQ), 1)) // SEL_LEN
    valid = j_idx <= cur
    forced = (j_idx == 0) | (j_idx == cur) | (j_idx == cur - 1)
    sub = lax.broadcasted_iota(jnp.int32, (8, TQ), 0)
    nch = nsb // 8
    for g in groups:
        score = jnp.where(valid, imp_t[g] + jnp.where(forced, 1e6, 0.0), -1e9)
        chunks = [score[8 * v:8 * v + 8] for v in range(nch)]
        counts = [jnp.zeros((8, TQ), F32) for _ in range(nch)]
        for i in range(nsb):
            row = jnp.broadcast_to(score[i:i + 1], (8, TQ))
            for v in range(nch):
                if i < 8 * v:
                    ahead = row >= chunks[v]
                elif i >= 8 * v + 8:
                    ahead = row > chunks[v]
                else:
                    ahead = jnp.where(sub > (i - 8 * v),
                                      jnp.where(row >= chunks[v], 1.0, 0.0),
                                      jnp.where(row > chunks[v], 1.0, 0.0)) > 0.5
                counts[v] = counts[v] + jnp.where(ahead, 1.0, 0.0)
        bias_t = jnp.concatenate(
            [jnp.where((c < float(N_SEL)) & valid[8 * v:8 * v + 8], 0.0, NEG) for v, c in enumerate(counts)]
            + [jnp.zeros((LANES - nsb, TQ), F32)], axis=0).astype(BF16)
        rhs_ref[g] = jnp.concatenate([qgs[g], tile4(bias_t)], axis=0)

    m_ref[...] = jnp.full(m_ref.shape, NEG, F32)
    l_ref[...] = jnp.zeros(l_ref.shape, F32)
    acc_ref[...] = jnp.zeros(acc_ref.shape, F32)

    def sel_tile(kt, diagonal):
        k0 = pl.multiple_of(kt * TK_SEL, TK_SEL)
        lhs = jnp.concatenate([ks_ref[0, pl.ds(k0, TK_SEL), :], et_ref[pl.ds(k0, TK_SEL), :]], axis=1)
        vt = vst_ref[0, :, pl.ds(k0, TK_SEL)]
        if diagonal:
            kpos = k0 + lax.broadcasted_iota(jnp.int32, (TK_SEL, TQ), 0)
            tpos = q0 + lax.broadcasted_iota(jnp.int32, (TK_SEL, TQ), 1)
            causal = tile4(kpos <= tpos)
        for g in groups:
            sc = _dot(lhs, rhs_ref[g])
            if diagonal:
                sc = jnp.where(causal, sc, NEG)
            m_i = m_ref[g]
            m_new = jnp.maximum(m_i, _col_reduce(sc, jnp.max))
            alpha = jnp.exp2(m_i - m_new)
            pexp = jnp.exp2(sc - m_new)
            l_ref[g] = alpha * l_ref[g] + _col_reduce(pexp, jnp.sum)
            acc_ref[g] = alpha * acc_ref[g] + _dot(vt, pexp.astype(BF16))
            m_ref[g] = m_new

    last = q0 // TK_SEL

    def sel_body(kt, carry):
        sel_tile(kt, False)
        return carry

    lax.fori_loop(0, last, sel_body, 0)
    sel_tile(last, True)

    k0 = pl.multiple_of(jnp.maximum(q0 - WINDOW, 0), TQ)
    kpos = k0 + lax.broadcasted_iota(jnp.int32, (wkeys, TQ), 0)
    tpos = q0 + lax.broadcasted_iota(jnp.int32, (wkeys, TQ), 1)
    wmask = tile4((kpos <= tpos) & (kpos > tpos - WINDOW))
    kwin = kw_ref[0, pl.ds(k0, wkeys), :]
    vwin = vwt_ref[0, :, pl.ds(k0, wkeys)]
    o_w = []
    for g in groups:
        sc = jnp.where(wmask, _dot(kwin, qgs[g]), NEG)
        pexp = jnp.exp2(sc - _col_reduce(sc, jnp.max))
        o_w.append(_dot(vwin, pexp.astype(BF16)) / _col_reduce(pexp, jnp.sum))

    blocks = []
    for g in groups:
        o_s = acc_ref[g] / l_ref[g]
        for h in range(HPG):
            r0 = LANES * g + 3 * h
            cols = slice(h * TQ, (h + 1) * TQ)
            og = (gt_ref[0, r0:r0 + 1, :] * o_c[g][:, cols] + gt_ref[0, r0 + 1:r0 + 2, :] * o_s[:, cols]
                  + gt_ref[0, r0 + 2:r0 + 3, :] * o_w[g][:, cols])
            blocks.append(og[HEAD_DIM * g:HEAD_DIM * (g + 1)])
    o_ref[0] = jnp.concatenate(blocks, axis=0).T.astype(o_ref.dtype)


def _attention(qt, kc, vct, kv, vt, gt, ovt, et):
    b, _, s = qt.shape
    ncp = kc.shape[1]
    kspec = lambda j: pl.BlockSpec((1, s, LANES), lambda bi, qi: (bi, 0, j))
    vspec = lambda j: pl.BlockSpec((1, LANES, s), lambda bi, qi: (bi, j, 0))
    return pl.pallas_call(
        _attn_kernel,
        grid=(b, s // TQ),
        in_specs=[pl.BlockSpec((1, N_HEADS * LANES, TQ), lambda bi, qi: (bi, 0, qi)),
                  pl.BlockSpec((1, ncp, LANES), lambda bi, qi: (bi, 0, 0)),
                  pl.BlockSpec((1, LANES, ncp), lambda bi, qi: (bi, 0, 0)),
                  kspec(2), vspec(0), kspec(4), vspec(1),
                  pl.BlockSpec((1, N_KV * LANES, TQ), lambda bi, qi: (bi, 0, qi)),
                  pl.BlockSpec(ovt.shape, lambda bi, qi: (0, 0)),
                  pl.BlockSpec(et.shape, lambda bi, qi: (0, 0))],
        out_specs=pl.BlockSpec((1, TQ, N_HEADS * HEAD_DIM), lambda bi, qi: (bi, qi, 0)),
        out_shape=jax.ShapeDtypeStruct((b, s, N_HEADS * HEAD_DIM), BF16),
        scratch_shapes=[pltpu.VMEM((N_KV, 2 * LANES, HPG * TQ), BF16),
                        pltpu.VMEM((N_KV, 1, HPG * TQ), F32), pltpu.VMEM((N_KV, 1, HPG * TQ), F32),
                        pltpu.VMEM((N_KV, LANES, HPG * TQ), F32)],
        compiler_params=_params(("parallel", "arbitrary")),
        name="nsa_attention",
    )(qt, kc, vct, kv, vt, kv, vt, gt, ovt, et)


def _conv_kernel(c_ref, halo_ref, cw_ref, cb_ref, g_ref, b_ref, o_ref, xs_ref):
    i = pl.program_id(1)
    ts = c_ref.shape[1]
    xs_ref[0:CONV_HALO, :] = jnp.where(i > 0, halo_ref[0], 0.0)
    xs_ref[CONV_HALO:CONV_HALO + ts, :] = c_ref[0]
    acc = jnp.zeros((ts, CONV_CH), F32)
    for w in range(CONV_WIDTH):
        off = CONV_HALO - (CONV_WIDTH - 1) + w
        acc = acc + xs_ref[off:off + ts, :] * cw_ref[w:w + 1, :]
    y = _layer_norm(acc + cb_ref[...], g_ref[...], b_ref[...])
    o_ref[0] = _silu(y).astype(o_ref.dtype)


def _conv_branch(c, cw, cb, ln_g, ln_b):
    b, s, ch = c.shape
    ts = TS_CONV
    per = ts // CONV_HALO
    vec = pl.BlockSpec((1, ch), lambda bi, i: (0, 0))
    return pl.pallas_call(
        _conv_kernel,
        grid=(b, s // ts),
        in_specs=[pl.BlockSpec((1, ts, ch), lambda bi, i: (bi, i, 0)),
                  pl.BlockSpec((1, CONV_HALO, ch), lambda bi, i: (bi, jnp.maximum(i * per - 1, 0), 0)),
                  pl.BlockSpec((CONV_WIDTH, ch), lambda bi, i: (0, 0)), vec, vec, vec],
        out_specs=pl.BlockSpec((1, ts, ch), lambda bi, i: (bi, i, 0)),
        out_shape=jax.ShapeDtypeStruct((b, s, ch), BF16),
        scratch_shapes=[pltpu.VMEM((CONV_HALO + ts, ch), F32)],
        compiler_params=_params(("parallel", "arbitrary")),
        name="conv_branch",
    )(c, c, cw, cb, ln_g, ln_b)


def _outproj_kernel(oa_ref, cc_ref, wt_ref, wb_ref, h_ref, gpost_ref, gnext_ref, h1_ref, u_ref):
    m = _dot(oa_ref[...], wt_ref[...]) + _dot(cc_ref[...], wb_ref[...])
    h1 = h_ref[...] + _rms(m, gpost_ref[...])
    h1_ref[...] = h1
    u_ref[...] = _rms(h1, gnext_ref[...]).astype(u_ref.dtype)


def _outproj(oa, cc, wt, wb, h, gpost, gnext):
    n, d = h.shape
    tm = TM_PROJ
    full = lambda a: pl.BlockSpec(a.shape, lambda i: (0,) * a.ndim)
    row = lambda c: pl.BlockSpec((tm, c), lambda i: (i, 0))
    return pl.pallas_call(
        _outproj_kernel,
        grid=(n // tm,),
        in_specs=[row(oa.shape[1]), row(cc.shape[1]), full(wt), full(wb), row(d), full(gpost), full(gnext)],
        out_specs=[row(d), row(d)],
        out_shape=[jax.ShapeDtypeStruct((n, d), F32), jax.ShapeDtypeStruct((n, d), BF16)],
        compiler_params=_params(("parallel",)),
        name="outproj",
    )(oa, cc, wt, wb, h, gpost, gnext)


def _ffn_kernel(u_ref, wg_ref, wu_ref, wd_ref, h_ref, gpost_ref, gnext_ref, h2_ref, u2_ref, acc_ref):
    f = pl.program_id(1)

    @pl.when(f == 0)
    def _():
        acc_ref[...] = jnp.zeros_like(acc_ref)

    u = u_ref[...]
    act = (_silu(_dot(u, wg_ref[...])) * _dot(u, wu_ref[...])).astype(BF16)
    acc_ref[...] += _dot(act, wd_ref[...])

    @pl.when(f == pl.num_programs(1) - 1)
    def _():
        h2 = h_ref[...] + _rms(acc_ref[...], gpost_ref[...])
        h2_ref[...] = h2
        u2_ref[...] = _rms(h2, gnext_ref[...]).astype(u2_ref.dtype)


def _ffn(u, wg, wu, wd, h, gpost, gnext):
    n, d = h.shape
    ff = wg.shape[1]
    tm, tf = TM_FFN, TF_FFN
    row = pl.BlockSpec((tm, d), lambda i, f: (i, 0))
    vec = pl.BlockSpec((1, d), lambda i, f: (0, 0))
    return pl.pallas_call(
        _ffn_kernel,
        grid=(n // tm, ff // tf),
        in_specs=[row, pl.BlockSpec((d, tf), lambda i, f: (0, f)), pl.BlockSpec((d, tf), lambda i, f: (0, f)),
                  pl.BlockSpec((tf, d), lambda i, f: (f, 0)), row, vec, vec],
        out_specs=[row, row],
        out_shape=[jax.ShapeDtypeStruct((n, d), F32), jax.ShapeDtypeStruct((n, d), BF16)],
        scratch_shapes=[pltpu.VMEM((tm, d), F32)],
        compiler_params=_params(("parallel", "arbitrary")),
        name="dense_ffn",
    )(u, wg, wu, wd, h, gpost, gnext)


def _gmlp_kernel(u_ref, win_ref, lng_ref, lnb_ref, ws_ref, bs_ref, wout_ref, h_ref, gpost_ref, gnext_ref,
                 wrh_ref, wrl_ref, h3_ref, u4_ref, route_ref, mix_ref):
    tm = u_ref.shape[0]
    width = wout_ref.shape[0]
    z = _gelu_tanh(_dot(u_ref[...], win_ref[...]))
    z1 = z[:, :width]
    z2 = _layer_norm(z[:, width:], lng_ref[...], lnb_ref[...]).astype(BF16)
    r_idx = lax.broadcasted_iota(jnp.int32, (GMLP_CHUNK, GMLP_CHUNK), 0)
    c_idx = lax.broadcasted_iota(jnp.int32, (GMLP_CHUNK, GMLP_CHUNK), 1)
    gw = width // GMLP_GROUPS
    for g in range(GMLP_GROUPS):
        wsg = jnp.where(c_idx <= r_idx, ws_ref[g], 0.0).astype(BF16)
        for c in range(tm // GMLP_CHUNK):
            rows = slice(c * GMLP_CHUNK, (c + 1) * GMLP_CHUNK)
            cols = slice(g * gw, (g + 1) * gw)
            mix_ref[rows, cols] = _dot(wsg, z2[rows, cols]) + bs_ref[:, cols]
    gated = (z1 * mix_ref[...]).astype(BF16)
    y = _dot(gated, wout_ref[...])
    h3 = h_ref[...] + _rms(y, gpost_ref[...])
    h3_ref[...] = h3
    u4 = _rms(h3, gnext_ref[...])
    u4_ref[...] = u4

    u_hi = u4.astype(BF16)
    u_lo = (u4 - u_hi.astype(F32)).astype(BF16)
    logits = _dot(u_hi, wrh_ref[...]) + (_dot(u_lo, wrh_ref[...]) + _dot(u_hi, wrl_ref[...]))
    lane = lax.broadcasted_iota(jnp.int32, (tm, LANES), 1).astype(F32)
    lg = jnp.where(lane < float(N_EXPERTS), logits, NEG)
    m1 = jnp.max(lg, axis=-1, keepdims=True)
    i1 = jnp.min(jnp.where(lg == m1, lane, float(LANES)), axis=-1, keepdims=True)
    lg2 = jnp.where(lane == i1, NEG, lg)
    m2 = jnp.max(lg2, axis=-1, keepdims=True)
    i2 = jnp.min(jnp.where(lg2 == m2, lane, float(LANES)), axis=-1, keepdims=True)
    e2 = jnp.exp(m2 - m1)
    den = 1.0 + e2
    route_ref[...] = jnp.where(lane == 0.0, i1, jnp.where(lane == 1.0, i2,
                               jnp.where(lane == 2.0, 1.0 / den, jnp.where(lane == 3.0, e2 / den, 0.0))))


def _gmlp(u, win, lng, lnb, ws, bsb, wout, h, gpost, gnext, wrh, wrl):
    n, d = h.shape
    tm = TM_PROJ
    full = lambda a: pl.BlockSpec(a.shape, lambda i: (0,) * a.ndim)
    row = lambda c: pl.BlockSpec((tm, c), lambda i: (i, 0))
    return pl.pallas_call(
        _gmlp_kernel,
        grid=(n // tm,),
        in_specs=[row(d), full(win), full(lng), full(lnb), full(ws), full(bsb), full(wout), row(d),
                  full(gpost), full(gnext), full(wrh), full(wrl)],
        out_specs=[row(d), row(d), row(LANES)],
        out_shape=[jax.ShapeDtypeStruct((n, d), F32), jax.ShapeDtypeStruct((n, d), F32),
                   jax.ShapeDtypeStruct((n, LANES), F32)],
        scratch_shapes=[pltpu.VMEM((tm, wout.shape[0]), F32)],
        compiler_params=_params(("parallel",)),
        name="gmlp_router",
    )(u, win, lng, lnb, ws, bsb, wout, h, gpost, gnext, wrh, wrl)


def _expert_kernel(tile_e_ref, nused_ref, rowtok_ref, x_hbm, wg_ref, wu_ref, wd_ref, y_ref,
                   xg_ref, xb_ref, acc_ref, sem):
    i = pl.program_id(0)
    f = pl.program_id(1)
    tm = xg_ref.shape[1]
    nused = nused_ref[0]
    active = i < nused
    slot = i % 2

    def row_copy(tok, s, r):
        return pltpu.make_async_copy(x_hbm.at[pl.ds(tok, 1)], xg_ref.at[s, pl.ds(r, 1)], sem.at[s])

    def issue_tile(t, s):
        def body(r, carry):
            row_copy(rowtok_ref[t * tm + r], s, r).start()
            return carry

        lax.fori_loop(0, tm, body, 0, unroll=8)

    def wait_tile(s):
        def body(r, carry):
            row_copy(0, s, r).wait()
            return carry

        lax.fori_loop(0, tm, body, 0, unroll=8)

    @pl.when((i == 0) & (f == 0))
    def _():
        issue_tile(0, 0)

    @pl.when(active & (f == 0))
    def _():
        wait_tile(slot)
        xb_ref[...] = xg_ref[slot].astype(BF16)
        acc_ref[...] = jnp.zeros_like(acc_ref)

        @pl.when(i + 1 < nused)
        def _():
            issue_tile(i + 1, 1 - slot)

    @pl.when(active)
    def _():
        x = xb_ref[...]
        act = (_silu(_dot(x, wg_ref[0])) * _dot(x, wu_ref[0])).astype(BF16)
        acc_ref[...] += _dot(act, wd_ref[0])

    last = f == pl.num_programs(1) - 1

    @pl.when(active & last)
    def _():
        y_ref[...] = acc_ref[...]

    @pl.when(jnp.logical_not(active) & last)
    def _():
        y_ref[...] = jnp.zeros_like(y_ref)


def _experts(tile_e, nused, rowtok, x, wg, wu, wd):
    n_rows = rowtok.shape[0]
    d = x.shape[1]
    ff = wg.shape[2]
    tm, tf = TM_MOE, TF_MOE
    nf = ff // tf

    def fidx(i, f, te, nu):
        return jnp.where(i < nu[0], f, nf - 1)

    grid_spec = pltpu.PrefetchScalarGridSpec(
        num_scalar_prefetch=3,
        grid=(n_rows // tm, nf),
        in_specs=[pl.BlockSpec(memory_space=pl.ANY),
                  pl.BlockSpec((1, d, tf), lambda i, f, te, nu, rt: (te[i], 0, fidx(i, f, te, nu))),
                  pl.BlockSpec((1, d, tf), lambda i, f, te, nu, rt: (te[i], 0, fidx(i, f, te, nu))),
                  pl.BlockSpec((1, tf, d), lambda i, f, te, nu, rt: (te[i], fidx(i, f, te, nu), 0))],
        out_specs=pl.BlockSpec((tm, d), lambda i, f, te, nu, rt: (i, 0)),
        scratch_shapes=[pltpu.VMEM((2, tm, d), F32), pltpu.VMEM((tm, d), BF16), pltpu.VMEM((tm, d), F32),
                        pltpu.SemaphoreType.DMA((2,))],
    )
    return pl.pallas_call(
        _expert_kernel,
        grid_spec=grid_spec,
        out_shape=jax.ShapeDtypeStruct((n_rows, d), F32),
        compiler_params=_params(("arbitrary", "arbitrary")),
        name="moe_experts",
    )(tile_e, nused, rowtok, x, wg, wu, wd)


def _combine_kernel(dest_ref, y_hbm, route_ref, h_ref, gpost_ref, o_ref, ya_ref, yb_ref, sem):
    i = pl.program_id(0)
    tc = ya_ref.shape[1]
    slot = i % 2

    def copies(d0, d1, s, r):
        return (pltpu.make_async_copy(y_hbm.at[pl.ds(d0, 1)], ya_ref.at[s, pl.ds(r, 1)], sem.at[0, s]),
                pltpu.make_async_copy(y_hbm.at[pl.ds(d1, 1)], yb_ref.at[s, pl.ds(r, 1)], sem.at[1, s]))

    def issue_tile(t, s):
        def body(r, carry):
            p = 2 * (t * tc + r)
            ca, cb = copies(dest_ref[p], dest_ref[p + 1], s, r)
            ca.start()
            cb.start()
            return carry

        lax.fori_loop(0, tc, body, 0, unroll=8)

    def wait_tile(s):
        def body(r, carry):
            ca, cb = copies(0, 0, s, r)
            ca.wait()
            cb.wait()
            return carry

        lax.fori_loop(0, tc, body, 0, unroll=8)

    @pl.when(i == 0)
    def _():
        issue_tile(0, 0)

    @pl.when(i + 1 < pl.num_programs(0))
    def _():
        issue_tile(i + 1, 1 - slot)

    wait_tile(slot)
    route = route_ref[...]
    moe = route[:, 2:3] * ya_ref[slot] + route[:, 3:4] * yb_ref[slot]
    o_ref[...] = h_ref[...] + _rms(moe, gpost_ref[...])


def _combine(dest, y, route, h, gpost):
    n, d = h.shape
    tc = TC_COMB
    grid_spec = pltpu.PrefetchScalarGridSpec(
        num_scalar_prefetch=1,
        grid=(n // tc,),
        in_specs=[pl.BlockSpec(memory_space=pl.ANY),
                  pl.BlockSpec((tc, LANES), lambda i, ds: (i, 0)),
                  pl.BlockSpec((tc, d), lambda i, ds: (i, 0)),
                  pl.BlockSpec((1, d), lambda i, ds: (0, 0))],
        out_specs=pl.BlockSpec((tc, d), lambda i, ds: (i, 0)),
        scratch_shapes=[pltpu.VMEM((2, tc, d), F32), pltpu.VMEM((2, tc, d), F32),
                        pltpu.SemaphoreType.DMA((2, 2))],
    )
    return pl.pallas_call(
        _combine_kernel,
        grid_spec=grid_spec,
        out_shape=jax.ShapeDtypeStruct((n, d), F32),
        compiler_params=_params(("arbitrary",)),
        name="moe_combine",
    )(dest, y, route, h, gpost)


def _overlap_table_t(seq):
    nc = (seq - CMP_LEN) // CMP_STRIDE + 1
    ncp = seq // CMP_STRIDE
    nsb = seq // SEL_LEN
    cs = np.arange(ncp)[None, :] * CMP_STRIDE
    js = np.arange(LANES)[:, None] * SEL_LEN
    ov = (cs < js + SEL_LEN) & (cs + CMP_LEN > js) & (np.arange(ncp)[None, :] < nc) & (np.arange(LANES)[:, None] < nsb)
    return jnp.asarray(ov.astype(np.float32), BF16)


def _expand_table_t(seq):
    e = (np.arange(seq)[:, None] // SEL_LEN) == np.arange(LANES)[None, :]
    return jnp.asarray(e.astype(np.float32), BF16)


def _layer0_weights(w_in):
    d = w_in.shape[0]
    qc = N_HEADS * HEAD_DIM
    kvc = 6 * N_KV * HEAD_DIM
    gc = 3 * N_HEADS
    gh = N_KV * HEAD_DIM
    wq = (w_in[:, :qc] * (HEAD_DIM ** -0.5 * np.log2(np.e))).reshape(d, N_KV, HPG, HEAD_DIM)
    zeros = jnp.zeros_like(wq[:, 0])
    wq_pad = jnp.stack([jnp.concatenate([wq[:, 0], zeros], axis=-1),
                        jnp.concatenate([zeros, wq[:, 1]], axis=-1)], axis=1)
    wqt = wq_pad.reshape(d, N_HEADS * LANES).T.astype(BF16)
    wkv = w_in[:, qc:qc + kvc]
    wvt = jnp.concatenate([wkv[:, 3 * gh:4 * gh], wkv[:, 5 * gh:6 * gh]], axis=1).T.astype(BF16)
    wg = w_in[:, qc + kvc:qc + kvc + gc].reshape(d, N_KV, 3 * HPG)
    wgt = jnp.pad(wg, ((0, 0), (0, 0), (0, LANES - 3 * HPG))).reshape(d, N_KV * LANES).T.astype(BF16)
    wglu = w_in[:, qc + kvc + gc:].astype(BF16)
    return wqt, wkv.astype(BF16), wvt, wgt, wglu


def _dispatch_plan(route, n_tok):
    tm = TM_MOE
    e_flat = route[:, 0:2].astype(jnp.int32).reshape(-1)
    onehot = (e_flat[:, None] == jnp.arange(N_EXPERTS, dtype=jnp.int32)[None, :]).astype(jnp.int32)
    rank = jnp.sum((jnp.cumsum(onehot, axis=0) - onehot) * onehot, axis=1)
    counts = jnp.sum(onehot, axis=0)
    padded = (counts + tm - 1) // tm * tm
    pad_ends = jnp.cumsum(padded)
    pad_starts = pad_ends - padded
    dest = (pad_starts[e_flat] + rank).astype(jnp.int32)
    n_rows = 2 * n_tok + N_EXPERTS * tm
    n_tiles = n_rows // tm
    rowtok = jnp.zeros((n_rows,), jnp.int32).at[dest].set(jnp.arange(2 * n_tok, dtype=jnp.int32) // 2)
    nused = (pad_ends[-1] // tm).astype(jnp.int32)
    tiles = jnp.minimum(jnp.arange(n_tiles, dtype=jnp.int32), nused - 1) * tm
    tile_e = jnp.minimum(jnp.sum((pad_ends[None, :] <= tiles[:, None]).astype(jnp.int32), axis=1), N_EXPERTS - 1)
    return tile_e, nused.reshape(1), rowtok, dest


def kernel(x, norm_mix_pre, norm_mix_post, norm_ffn_pre, norm_ffn_post, nsa_conv_w_in, cmp_pe, cmp_w1, cmp_b1, cmp_w2, cmp_b2, conv_w, conv_b, conv_ln_g, conv_ln_b, nsa_conv_w_out, ffn_w_gate, ffn_w_up, ffn_w_down, gmlp_w_in, gmlp_ln_g, gmlp_ln_b, gmlp_w_s, gmlp_b_s, gmlp_w_out, moe_w_router, moe_w_gate, moe_w_up, moe_w_down):
    b, s, d = x.shape
    n = b * s
    assert d == D_MODEL and s // SEL_LEN == SEL_LEN and s % (2 * TK_SEL) == 0 and n % TM_PROJ == 0
    h0 = x.reshape(n, d)
    vec = lambda a: a.reshape(1, -1).astype(F32)

    wqt, wkv, wvt, wgt, wglu = _layer0_weights(nsa_conv_w_in[0])
    qt, kv, vt, gt, c = _proj0(h0, vec(norm_mix_pre[0]), wqt, wkv, wvt, wgt, wglu, b)
    kv3 = kv.reshape(b, s, 6 * N_KV * HEAD_DIM)

    nr = s // CMP_STRIDE
    kr = kv3[:, :, :2 * N_KV * HEAD_DIM].reshape(b, nr, CMP_STRIDE, 2, N_KV, HEAD_DIM)
    kr = kr.transpose(3, 0, 4, 1, 2, 5).reshape(2, b * N_KV, nr, CMP_STRIDE * HEAD_DIM)
    cmp = _compress(kr, cmp_pe[0].reshape(2, 1, CMP_LEN * HEAD_DIM).astype(F32), cmp_w1[0].astype(BF16),
                    cmp_b1[0].reshape(2, 1, CMP_HIDDEN).astype(F32), cmp_w2[0].astype(BF16),
                    cmp_b2[0].reshape(2, 1, HEAD_DIM).astype(F32))
    cmp = cmp.reshape(2, b, N_KV, nr, HEAD_DIM)
    kc = cmp[0].transpose(0, 2, 1, 3).reshape(b, nr, N_KV * HEAD_DIM)
    vct = cmp[1].transpose(0, 1, 3, 2).reshape(b, N_KV * HEAD_DIM, nr)

    o_attn = _attention(qt, kc, vct, kv3, vt, gt, _overlap_table_t(s), _expand_table_t(s))
    cc = _conv_branch(c.reshape(b, s, CONV_CH), conv_w[0].reshape(CONV_WIDTH, CONV_CH).astype(F32),
                      vec(conv_b[0]), vec(conv_ln_g[0]), vec(conv_ln_b[0]))
    w_out = nsa_conv_w_out[0].astype(BF16)
    nw = N_HEADS * HEAD_DIM
    h1, u1 = _outproj(o_attn.reshape(n, nw), cc.reshape(n, CONV_CH), w_out[:nw], w_out[nw:], h0,
                      vec(norm_mix_post[0]), vec(norm_ffn_pre[0]))
    h2, u2 = _ffn(u1, ffn_w_gate[0].astype(BF16), ffn_w_up[0].astype(BF16), ffn_w_down[0].astype(BF16), h1,
                  vec(norm_ffn_post[0]), vec(norm_mix_pre[1]))

    bsb = jnp.repeat(gmlp_b_s[0].T.astype(F32), d // GMLP_GROUPS, axis=1)
    wr = jnp.pad(moe_w_router[0].astype(F32), ((0, 0), (0, LANES - N_EXPERTS)))
    wr_hi = wr.astype(BF16)
    wr_lo = (wr - wr_hi.astype(F32)).astype(BF16)
    h3, u4, route = _gmlp(u2, gmlp_w_in[0].astype(BF16), vec(gmlp_ln_g[0]), vec(gmlp_ln_b[0]),
                          gmlp_w_s[0].astype(F32), bsb, gmlp_w_out[0].astype(BF16), h2,
                          vec(norm_mix_post[1]), vec(norm_ffn_pre[1]), wr_hi, wr_lo)
    tile_e, nused, rowtok, dest = _dispatch_plan(route, n)
    y = _experts(tile_e, nused, rowtok, u4, moe_w_gate[0].astype(BF16), moe_w_up[0].astype(BF16),
                 moe_w_down[0].astype(BF16))
    h4 = _combine(dest, y, route, h3, vec(norm_ffn_post[1]))
    return h4.reshape(b, s, d)
```

```python
import functools

import numpy as np
import jax
import jax.numpy as jnp
from jax import lax
from jax.experimental import pallas as pl
from jax.experimental.pallas import tpu as pltpu

F32 = jnp.float32
BF16 = jnp.bfloat16

D_MODEL = 1024
N_HEADS = 8
N_KV = 2
HPG = N_HEADS // N_KV
HEAD_DIM = 64
CMP_LEN = 32
CMP_STRIDE = 16
CMP_HIDDEN = 256
SEL_LEN = 64
N_SEL = 16
WINDOW = 512
CONV_CH = D_MODEL // 2
CONV_WIDTH = 31
GMLP_GROUPS = 8
GMLP_CHUNK = 128
N_EXPERTS = 8
EPS = 1e-6
NEG = -1e30

LANES = 128
TQ = 128
TK_SEL = 1024
TM_PROJ = 512
TS_CONV = 512
CONV_HALO = 32
TM_FFN = 512
TF_FFN = 1408
TM_MOE = 512
TF_MOE = 1792
TC_COMB = 256
VMEM_LIMIT = 56 * 1024 * 1024


def _params(sem):
    return pltpu.CompilerParams(dimension_semantics=sem, vmem_limit_bytes=VMEM_LIMIT)


def _dot(a, b):
    return jnp.dot(a, b, preferred_element_type=F32)


def _dot_nt(a, b):
    return lax.dot_general(a, b, (((1,), (1,)), ((), ())), preferred_element_type=F32)


def _rms(x, g):
    return x * lax.rsqrt(jnp.mean(x * x, axis=-1, keepdims=True) + EPS) * g


def _layer_norm(x, g, b):
    mu = jnp.mean(x, axis=-1, keepdims=True)
    xc = x - mu
    var = jnp.mean(xc * xc, axis=-1, keepdims=True)
    return xc * lax.rsqrt(var + EPS) * g + b


def _sigmoid(x):
    return 1.0 / (1.0 + jnp.exp(-x))


def _silu(x):
    return x * _sigmoid(x)


def _gelu_tanh(x):
    c = np.float32(np.sqrt(2.0 / np.pi))
    return 0.5 * x * (1.0 + jnp.tanh(c * (x + 0.044715 * (x * x * x))))


def _proj0_kernel(h_ref, g_ref, wqt_ref, wkv_ref, wvt_ref, wgt_ref, wglu_ref,
                  qt_ref, kv_ref, vt_ref, gt_ref, c_ref):
    u = _rms(h_ref[...], g_ref[...]).astype(BF16)
    qt_ref[0] = _dot_nt(wqt_ref[...], u).astype(BF16)
    kv_ref[...] = _dot(u, wkv_ref[...]).astype(BF16)
    vt_ref[0] = _dot_nt(wvt_ref[...], u).astype(BF16)
    gt_ref[0] = _sigmoid(_dot_nt(wgt_ref[...], u))
    glu = _dot(u, wglu_ref[...])
    c_ref[...] = glu[:, :CONV_CH] * _sigmoid(glu[:, CONV_CH:])


def _proj0(h, g, wqt, wkv, wvt, wgt, wglu, batch):
    n, d = h.shape
    tm = TM_PROJ
    seq = n // batch
    per = seq // tm
    full = lambda a: pl.BlockSpec(a.shape, lambda i: (0,) * a.ndim)
    row = lambda c: pl.BlockSpec((tm, c), lambda i: (i, 0))
    col = lambda r: pl.BlockSpec((1, r, tm), lambda i: (i // per, 0, i % per))
    return pl.pallas_call(
        _proj0_kernel,
        grid=(n // tm,),
        in_specs=[row(d), full(g), full(wqt), full(wkv), full(wvt), full(wgt), full(wglu)],
        out_specs=[col(wqt.shape[0]), row(wkv.shape[1]), col(wvt.shape[0]), col(wgt.shape[0]), row(CONV_CH)],
        out_shape=[jax.ShapeDtypeStruct((batch, wqt.shape[0], seq), BF16),
                   jax.ShapeDtypeStruct((n, wkv.shape[1]), BF16),
                   jax.ShapeDtypeStruct((batch, wvt.shape[0], seq), BF16),
                   jax.ShapeDtypeStruct((batch, wgt.shape[0], seq), F32),
                   jax.ShapeDtypeStruct((n, CONV_CH), F32)],
        compiler_params=_params(("parallel",)),
        name="proj0",
    )(h, g, wqt, wkv, wvt, wgt, wglu)


def _compress_kernel(kr_ref, pe_ref, w1_ref, b1_ref, w2_ref, b2_ref, o_ref):
    half = CMP_STRIDE * HEAD_DIM
    kr = kr_ref[0, 0].astype(F32)
    pe = pe_ref[0]
    top = (kr + pe[:, :half]).astype(BF16)
    bot = (kr + pe[:, half:]).astype(BF16)
    a = _dot(top, w1_ref[0, :half, :])
    b = _dot(bot, w1_ref[0, half:, :])
    nrow = b.shape[0]
    pre = a + pltpu.roll(b, nrow - 1, 0) + b1_ref[0]
    hid = _gelu_tanh(pre).astype(BF16)
    o_ref[0, 0] = (_dot(hid, w2_ref[0]) + b2_ref[0]).astype(o_ref.dtype)


def _compress(kr, pe, w1, b1, w2, b2):
    _, bg, nr, feat = kr.shape
    return pl.pallas_call(
        _compress_kernel,
        grid=(2, bg),
        in_specs=[pl.BlockSpec((1, 1, nr, feat), lambda j, i: (j, i, 0, 0)),
                  pl.BlockSpec((1, 1, 2 * feat), lambda j, i: (j, 0, 0)),
                  pl.BlockSpec((1, 2 * feat, CMP_HIDDEN), lambda j, i: (j, 0, 0)),
                  pl.BlockSpec((1, 1, CMP_HIDDEN), lambda j, i: (j, 0, 0)),
                  pl.BlockSpec((1, CMP_HIDDEN, HEAD_DIM), lambda j, i: (j, 0, 0)),
                  pl.BlockSpec((1, 1, HEAD_DIM), lambda j, i: (j, 0, 0))],
        out_specs=pl.BlockSpec((1, 1, nr, HEAD_DIM), lambda j, i: (j, i, 0, 0)),
        out_shape=jax.ShapeDtypeStruct((2, bg, nr, HEAD_DIM), BF16),
        compiler_params=_params(("parallel", "parallel")),
        name="compress",
    )(kr, pe, w1, b1, w2, b2)


def _col_reduce(x, op):
    rows = x.shape[0]
    part = op(x.reshape(4, rows // 4, x.shape[1]), axis=0)
    return op(part, axis=0, keepdims=True)


def _attn_kernel(qt_ref, kc_ref, vct_ref, ks_ref, vst_ref, kw_ref, vwt_ref, gt_ref,
                 ovt_ref, et_ref, o_ref, rhs_ref, m_ref, l_ref, acc_ref):
    qi = pl.program_id(1)
    q0 = qi * TQ
    nsb = SEL_LEN
    ncp = kc_ref.shape[1]
    wkeys = WINDOW + TQ
    groups = range(N_KV)
    tile4 = lambda a: jnp.concatenate([a] * HPG, axis=1)
    qgs = [jnp.concatenate(
        [qt_ref[0, LANES * (HPG * g + h):LANES * (HPG * g + h + 1), :] for h in range(HPG)], axis=1)
        for g in groups]

    n_idx = lax.broadcasted_iota(jnp.int32, (ncp, TQ), 0)
    t_idx = q0 + lax.broadcasted_iota(jnp.int32, (ncp, TQ), 1)
    cmask = tile4(n_idx * CMP_STRIDE + (CMP_LEN - 1) <= t_idx)
    o_c, imp_t = [], []
    for g in groups:
        s = jnp.where(cmask, _dot(kc_ref[0], qgs[g]), NEG)
        m = _col_reduce(s, jnp.max)
        p = jnp.where(cmask, jnp.exp2(s - m), 0.0)
        l = _col_reduce(p, jnp.sum)
        p = p / jnp.where(l > 0.0, l, 1.0)
        o_c.append(_dot(vct_ref[0], p.astype(BF16)))
        psum = p[:, 0:TQ] + p[:, TQ:2 * TQ] + p[:, 2 * TQ:3 * TQ] + p[:, 3 * TQ:4 * TQ]
        imp_t.append(_dot(ovt_ref[...], psum.astype(BF16))[:nsb])

    j_idx = lax.broadcasted_iota(jnp.int32, (nsb, TQ), 0)
    cur = (q0 + lax.broadcasted_iota(jnp.int32, (nsb, TQ), 1)) // SEL_LEN
    valid = j_idx <= cur
    forced = (j_idx == 0) | (j_idx == cur) | (j_idx == cur - 1)
    sub = lax.broadcasted_iota(jnp.int32, (8, TQ), 0)
    nch = nsb // 8
    for g in groups:
        score = jnp.where(valid, imp_t[g] + jnp.where(forced, 1e6, 0.0), -1e9)
        chunks = [score[8 * v:8 * v + 8] for v in range(nch)]
        counts = [jnp.zeros((8, TQ), F32) for _ in range(nch)]
        for i in range(nsb):
            row = jnp.broadcast_to(score[i:i + 1], (8, TQ))
            for v in range(nch):
                if i < 8 * v:
                    ahead = row >= chunks[v]
                elif i >= 8 * v + 8:
                    ahead = row > chunks[v]
                else:
                    ahead = jnp.where(sub > (i - 8 * v),
                                      jnp.where(row >= chunks[v], 1.0, 0.0),
                                      jnp.where(row > chunks[v], 1.0, 0.0)) > 0.5
                counts[v] = counts[v] + jnp.where(ahead, 1.0, 0.0)
        bias_t = jnp.concatenate(
            [jnp.where((c < float(N_SEL)) & valid[8 * v:8 * v + 8], 0.0, NEG) for v, c in enumerate(counts)]
            + [jnp.zeros((LANES - nsb, TQ), F32)], axis=0).astype(BF16)
        rhs_ref[g] = jnp.concatenate([qgs[g], tile4(bias_t)], axis=0)

    m_ref[...] = jnp.full(m_ref.shape, NEG, F32)
    l_ref[...] = jnp.zeros(l_ref.shape, F32)
    acc_ref[...] = jnp.zeros(acc_ref.shape, F32)

    def sel_tile(kt, diagonal):
        k0 = pl.multiple_of(kt * TK_SEL, TK_SEL)
        lhs = jnp.concatenate([ks_ref[0, pl.ds(k0, TK_SEL), :], et_ref[pl.ds(k0, TK_SEL), :]], axis=1)
        vt = vst_ref[0, :, pl.ds(k0, TK_SEL)]
        if diagonal:
            kpos = k0 + lax.broadcasted_iota(jnp.int32, (TK_SEL, TQ), 0)
            tpos = q0 + lax.broadcasted_iota(jnp.int32, (TK_SEL, TQ), 1)
            causal = tile4(kpos <= tpos)
        for g in groups:
            sc = _dot(lhs, rhs_ref[g])
            if diagonal:
                sc = jnp.where(causal, sc, NEG)
            m_i = m_ref[g]
            m_new = jnp.maximum(m_i, _col_reduce(sc, jnp.max))
            alpha = jnp.exp2(m_i - m_new)
            pexp = jnp.exp2(sc - m_new)
            l_ref[g] = alpha * l_ref[g] + _col_reduce(pexp, jnp.sum)
            acc_ref[g] = alpha * acc_ref[g] + _dot(vt, pexp.astype(BF16))
            m_ref[g] = m_new

    last = q0 // TK_SEL

    def sel_body(kt, carry):
        sel_tile(kt, False)
        return carry

    lax.fori_loop(0, last, sel_body, 0)
    sel_tile(last, True)

    k0 = pl.multiple_of(jnp.maximum(q0 - WINDOW, 0), TQ)
    kpos = k0 + lax.broadcasted_iota(jnp.int32, (wkeys, TQ), 0)
    tpos = q0 + lax.broadcasted_iota(jnp.int32, (wkeys, TQ), 1)
    wmask = tile4((kpos <= tpos) & (kpos > tpos - WINDOW))
    kwin = kw_ref[0, pl.ds(k0, wkeys), :]
    vwin = vwt_ref[0, :, pl.ds(k0, wkeys)]
    o_w = []
    for g in groups:
        sc = jnp.where(wmask, _dot(kwin, qgs[g]), NEG)
        pexp = jnp.exp2(sc - _col_reduce(sc, jnp.max))
        o_w.append(_dot(vwin, pexp.astype(BF16)) / _col_reduce(pexp, jnp.sum))

    blocks = []
    for g in groups:
        o_s = acc_ref[g] / l_ref[g]
        for h in range(HPG):
            r0 = LANES * g + 3 * h
            cols = slice(h * TQ, (h + 1) * TQ)
            og = (gt_ref[0, r0:r0 + 1, :] * o_c[g][:, cols] + gt_ref[0, r0 + 1:r0 + 2, :] * o_s[:, cols]
                  + gt_ref[0, r0 + 2:r0 + 3, :] * o_w[g][:, cols])
            blocks.append(og[HEAD_DIM * g:HEAD_DIM * (g + 1)])
    o_ref[0] = jnp.concatenate(blocks, axis=0).T.astype(o_ref.dtype)


def _attention(qt, kc, vct, kv, vt, gt, ovt, et):
    b, _, s = qt.shape
    ncp = kc.shape[1]
    kspec = lambda j: pl.BlockSpec((1, s, LANES), lambda bi, qi: (bi, 0, j))
    vspec = lambda j: pl.BlockSpec((1, LANES, s), lambda bi, qi: (bi, j, 0))
    return pl.pallas_call(
        _attn_kernel,
        grid=(b, s // TQ),
        in_specs=[pl.BlockSpec((1, N_HEADS * LANES, TQ), lambda bi, qi: (bi, 0, qi)),
                  pl.BlockSpec((1, ncp, LANES), lambda bi, qi: (bi, 0, 0)),
                  pl.BlockSpec((1, LANES, ncp), lambda bi, qi: (bi, 0, 0)),
                  kspec(2), vspec(0), kspec(4), vspec(1),
                  pl.BlockSpec((1, N_KV * LANES, TQ), lambda bi, qi: (bi, 0, qi)),
                  pl.BlockSpec(ovt.shape, lambda bi, qi: (0, 0)),
                  pl.BlockSpec(et.shape, lambda bi, qi: (0, 0))],
        out_specs=pl.BlockSpec((1, TQ, N_HEADS * HEAD_DIM), lambda bi, qi: (bi, qi, 0)),
        out_shape=jax.ShapeDtypeStruct((b, s, N_HEADS * HEAD_DIM), BF16),
        scratch_shapes=[pltpu.VMEM((N_KV, 2 * LANES, HPG * TQ), BF16),
                        pltpu.VMEM((N_KV, 1, HPG * TQ), F32), pltpu.VMEM((N_KV, 1, HPG * TQ), F32),
                        pltpu.VMEM((N_KV, LANES, HPG * TQ), F32)],
        compiler_params=_params(("parallel", "arbitrary")),
        name="nsa_attention",
    )(qt, kc, vct, kv, vt, kv, vt, gt, ovt, et)


def _conv_kernel(c_ref, halo_ref, cw_ref, cb_ref, g_ref, b_ref, o_ref, xs_ref):
    i = pl.program_id(1)
    ts = c_ref.shape[1]
    xs_ref[0:CONV_HALO, :] = jnp.where(i > 0, halo_ref[0], 0.0)
    xs_ref[CONV_HALO:CONV_HALO + ts, :] = c_ref[0]
    acc = jnp.zeros((ts, CONV_CH), F32)
    for w in range(CONV_WIDTH):
        off = CONV_HALO - (CONV_WIDTH - 1) + w
        acc = acc + xs_ref[off:off + ts, :] * cw_ref[w:w + 1, :]
    y = _layer_norm(acc + cb_ref[...], g_ref[...], b_ref[...])
    o_ref[0] = _silu(y).astype(o_ref.dtype)


def _conv_branch(c, cw, cb, ln_g, ln_b):
    b, s, ch = c.shape
    ts = TS_CONV
    per = ts // CONV_HALO
    vec = pl.BlockSpec((1, ch), lambda bi, i: (0, 0))
    return pl.pallas_call(
        _conv_kernel,
        grid=(b, s // ts),
        in_specs=[pl.BlockSpec((1, ts, ch), lambda bi, i: (bi, i, 0)),
                  pl.BlockSpec((1, CONV_HALO, ch), lambda bi, i: (bi, jnp.maximum(i * per - 1, 0), 0)),
                  pl.BlockSpec((CONV_WIDTH, ch), lambda bi, i: (0, 0)), vec, vec, vec],
        out_specs=pl.BlockSpec((1, ts, ch), lambda bi, i: (bi, i, 0)),
        out_shape=jax.ShapeDtypeStruct((b, s, ch), BF16),
        scratch_shapes=[pltpu.VMEM((CONV_HALO + ts, ch), F32)],
        compiler_params=_params(("parallel", "arbitrary")),
        name="conv_branch",
    )(c, c, cw, cb, ln_g, ln_b)


def _outproj_kernel(oa_ref, cc_ref, wt_ref, wb_ref, h_ref, gpost_ref, gnext_ref, h1_ref, u_ref):
    m = _dot(oa_ref[...], wt_ref[...]) + _dot(cc_ref[...], wb_ref[...])
    h1 = h_ref[...] + _rms(m, gpost_ref[...])
    h1_ref[...] = h1
    u_ref[...] = _rms(h1, gnext_ref[...]).astype(u_ref.dtype)


def _outproj(oa, cc, wt, wb, h, gpost, gnext):
    n, d = h.shape
    tm = TM_PROJ
    full = lambda a: pl.BlockSpec(a.shape, lambda i: (0,) * a.ndim)
    row = lambda c: pl.BlockSpec((tm, c), lambda i: (i, 0))
    return pl.pallas_call(
        _outproj_kernel,
        grid=(n // tm,),
        in_specs=[row(oa.shape[1]), row(cc.shape[1]), full(wt), full(wb), row(d), full(gpost), full(gnext)],
        out_specs=[row(d), row(d)],
        out_shape=[jax.ShapeDtypeStruct((n, d), F32), jax.ShapeDtypeStruct((n, d), BF16)],
        compiler_params=_params(("parallel",)),
        name="outproj",
    )(oa, cc, wt, wb, h, gpost, gnext)


def _ffn_kernel(u_ref, wg_ref, wu_ref, wd_ref, h_ref, gpost_ref, gnext_ref, h2_ref, u2_ref, acc_ref):
    f = pl.program_id(1)

    @pl.when(f == 0)
    def _():
        acc_ref[...] = jnp.zeros_like(acc_ref)

    u = u_ref[...]
    act = (_silu(_dot(u, wg_ref[...])) * _dot(u, wu_ref[...])).astype(BF16)
    acc_ref[...] += _dot(act, wd_ref[...])

    @pl.when(f == pl.num_programs(1) - 1)
    def _():
        h2 = h_ref[...] + _rms(acc_ref[...], gpost_ref[...])
        h2_ref[...] = h2
        u2_ref[...] = _rms(h2, gnext_ref[...]).astype(u2_ref.dtype)


def _ffn(u, wg, wu, wd, h, gpost, gnext):
    n, d = h.shape
    ff = wg.shape[1]
    tm, tf = TM_FFN, TF_FFN
    row = pl.BlockSpec((tm, d), lambda i, f: (i, 0))
    vec = pl.BlockSpec((1, d), lambda i, f: (0, 0))
    return pl.pallas_call(
        _ffn_kernel,
        grid=(n // tm, ff // tf),
        in_specs=[row, pl.BlockSpec((d, tf), lambda i, f: (0, f)), pl.BlockSpec((d, tf), lambda i, f: (0, f)),
                  pl.BlockSpec((tf, d), lambda i, f: (f, 0)), row, vec, vec],
        out_specs=[row, row],
        out_shape=[jax.ShapeDtypeStruct((n, d), F32), jax.ShapeDtypeStruct((n, d), BF16)],
        scratch_shapes=[pltpu.VMEM((tm, d), F32)],
        compiler_params=_params(("parallel", "arbitrary")),
        name="dense_ffn",
    )(u, wg, wu, wd, h, gpost, gnext)


def _gmlp_kernel(u_ref, win_ref, lng_ref, lnb_ref, ws_ref, bs_ref, wout_ref, h_ref, gpost_ref, gnext_ref,
                 wrh_ref, wrl_ref, h3_ref, u4_ref, route_ref, mix_ref):
    tm = u_ref.shape[0]
    width = wout_ref.shape[0]
    z = _gelu_tanh(_dot(u_ref[...], win_ref[...]))
    z1 = z[:, :width]
    z2 = _layer_norm(z[:, width:], lng_ref[...], lnb_ref[...]).astype(BF16)
    r_idx = lax.broadcasted_iota(jnp.int32, (GMLP_CHUNK, GMLP_CHUNK), 0)
    c_idx = lax.broadcasted_iota(jnp.int32, (GMLP_CHUNK, GMLP_CHUNK), 1)
    gw = width // GMLP_GROUPS
    for g in range(GMLP_GROUPS):
        wsg = jnp.where(c_idx <= r_idx, ws_ref[g], 0.0).astype(BF16)
        for c in range(tm // GMLP_CHUNK):
            rows = slice(c * GMLP_CHUNK, (c + 1) * GMLP_CHUNK)
            cols = slice(g * gw, (g + 1) * gw)
            mix_ref[rows, cols] = _dot(wsg, z2[rows, cols]) + bs_ref[:, cols]
    gated = (z1 * mix_ref[...]).astype(BF16)
    y = _dot(gated, wout_ref[...])
    h3 = h_ref[...] + _rms(y, gpost_ref[...])
    h3_ref[...] = h3
    u4 = _rms(h3, gnext_ref[...])
    u4_ref[...] = u4

    u_hi = u4.astype(BF16)
    u_lo = (u4 - u_hi.astype(F32)).astype(BF16)
    logits = _dot(u_hi, wrh_ref[...]) + (_dot(u_lo, wrh_ref[...]) + _dot(u_hi, wrl_ref[...]))
    lane = lax.broadcasted_iota(jnp.int32, (tm, LANES), 1).astype(F32)
    lg = jnp.where(lane < float(N_EXPERTS), logits, NEG)
    m1 = jnp.max(lg, axis=-1, keepdims=True)
    i1 = jnp.min(jnp.where(lg == m1, lane, float(LANES)), axis=-1, keepdims=True)
    lg2 = jnp.where(lane == i1, NEG, lg)
    m2 = jnp.max(lg2, axis=-1, keepdims=True)
    i2 = jnp.min(jnp.where(lg2 == m2, lane, float(LANES)), axis=-1, keepdims=True)
    e2 = jnp.exp(m2 - m1)
    den = 1.0 + e2
    route_ref[...] = jnp.where(lane == 0.0, i1, jnp.where(lane == 1.0, i2,
                               jnp.where(lane == 2.0, 1.0 / den, jnp.where(lane == 3.0, e2 / den, 0.0))))


def _gmlp(u, win, lng, lnb, ws, bsb, wout, h, gpost, gnext, wrh, wrl):
    n, d = h.shape
    tm = TM_PROJ
    full = lambda a: pl.BlockSpec(a.shape, lambda i: (0,) * a.ndim)
    row = lambda c: pl.BlockSpec((tm, c), lambda i: (i, 0))
    return pl.pallas_call(
        _gmlp_kernel,
        grid=(n // tm,),
        in_specs=[row(d), full(win), full(lng), full(lnb), full(ws), full(bsb), full(wout), row(d),
                  full(gpost), full(gnext), full(wrh), full(wrl)],
        out_specs=[row(d), row(d), row(LANES)],
        out_shape=[jax.ShapeDtypeStruct((n, d), F32), jax.ShapeDtypeStruct((n, d), F32),
                   jax.ShapeDtypeStruct((n, LANES), F32)],
        scratch_shapes=[pltpu.VMEM((tm, wout.shape[0]), F32)],
        compiler_params=_params(("parallel",)),
        name="gmlp_router",
    )(u, win, lng, lnb, ws, bsb, wout, h, gpost, gnext, wrh, wrl)


def _expert_kernel(tile_e_ref, nused_ref, rowtok_ref, x_hbm, wg_ref, wu_ref, wd_ref, y_ref,
                   xg_ref, xb_ref, acc_ref, sem):
    i = pl.program_id(0)
    f = pl.program_id(1)
    tm = xg_ref.shape[1]
    nused = nused_ref[0]
    active = i < nused
    slot = i % 2

    def row_copy(tok, s, r):
        return pltpu.make_async_copy(x_hbm.at[pl.ds(tok, 1)], xg_ref.at[s, pl.ds(r, 1)], sem.at[s])

    def issue_tile(t, s):
        def body(r, carry):
            row_copy(rowtok_ref[t * tm + r], s, r).start()
            return carry

        lax.fori_loop(0, tm, body, 0, unroll=8)

    def wait_tile(s):
        def body(r, carry):
            row_copy(0, s, r).wait()
            return carry

        lax.fori_loop(0, tm, body, 0, unroll=8)

    @pl.when((i == 0) & (f == 0))
    def _():
        issue_tile(0, 0)

    @pl.when(active & (f == 0))
    def _():
        wait_tile(slot)
        xb_ref[...] = xg_ref[slot].astype(BF16)
        acc_ref[...] = jnp.zeros_like(acc_ref)

        @pl.when(i + 1 < nused)
        def _():
            issue_tile(i + 1, 1 - slot)

    @pl.when(active)
    def _():
        x = xb_ref[...]
        act = (_silu(_dot(x, wg_ref[0])) * _dot(x, wu_ref[0])).astype(BF16)
        acc_ref[...] += _dot(act, wd_ref[0])

    last = f == pl.num_programs(1) - 1

    @pl.when(active & last)
    def _():
        y_ref[...] = acc_ref[...]

    @pl.when(jnp.logical_not(active) & last)
    def _():
        y_ref[...] = jnp.zeros_like(y_ref)


def _experts(tile_e, nused, rowtok, x, wg, wu, wd):
    n_rows = rowtok.shape[0]
    d = x.shape[1]
    ff = wg.shape[2]
    tm, tf = TM_MOE, TF_MOE
    nf = ff // tf

    def fidx(i, f, te, nu):
        return jnp.where(i < nu[0], f, nf - 1)

    grid_spec = pltpu.PrefetchScalarGridSpec(
        num_scalar_prefetch=3,
        grid=(n_rows // tm, nf),
        in_specs=[pl.BlockSpec(memory_space=pl.ANY),
                  pl.BlockSpec((1, d, tf), lambda i, f, te, nu, rt: (te[i], 0, fidx(i, f, te, nu))),
                  pl.BlockSpec((1, d, tf), lambda i, f, te, nu, rt: (te[i], 0, fidx(i, f, te, nu))),
                  pl.BlockSpec((1, tf, d), lambda i, f, te, nu, rt: (te[i], fidx(i, f, te, nu), 0))],
        out_specs=pl.BlockSpec((tm, d), lambda i, f, te, nu, rt: (i, 0)),
        scratch_shapes=[pltpu.VMEM((2, tm, d), F32), pltpu.VMEM((tm, d), BF16), pltpu.VMEM((tm, d), F32),
                        pltpu.SemaphoreType.DMA((2,))],
    )
    return pl.pallas_call(
        _expert_kernel,
        grid_spec=grid_spec,
        out_shape=jax.ShapeDtypeStruct((n_rows, d), F32),
        compiler_params=_params(("arbitrary", "arbitrary")),
        name="moe_experts",
    )(tile_e, nused, rowtok, x, wg, wu, wd)


def _combine_kernel(dest_ref, y_hbm, route_ref, h_ref, gpost_ref, o_ref, ya_ref, yb_ref, sem):
    i = pl.program_id(0)
    tc = ya_ref.shape[1]
    slot = i % 2

    def copies(d0, d1, s, r):
        return (pltpu.make_async_copy(y_hbm.at[pl.ds(d0, 1)], ya_ref.at[s, pl.ds(r, 1)], sem.at[0, s]),
                pltpu.make_async_copy(y_hbm.at[pl.ds(d1, 1)], yb_ref.at[s, pl.ds(r, 1)], sem.at[1, s]))

    def issue_tile(t, s):
        def body(r, carry):
            p = 2 * (t * tc + r)
            ca, cb = copies(dest_ref[p], dest_ref[p + 1], s, r)
            ca.start()
            cb.start()
            return carry

        lax.fori_loop(0, tc, body, 0, unroll=8)

    def wait_tile(s):
        def body(r, carry):
            ca, cb = copies(0, 0, s, r)
            ca.wait()
            cb.wait()
            return carry

        lax.fori_loop(0, tc, body, 0, unroll=8)

    @pl.when(i == 0)
    def _():
        issue_tile(0, 0)

    @pl.when(i + 1 < pl.num_programs(0))
    def _():
        issue_tile(i + 1, 1 - slot)

    wait_tile(slot)
    route = route_ref[...]
    moe = route[:, 2:3] * ya_ref[slot] + route[:, 3:4] * yb_ref[slot]
    o_ref[...] = h_ref[...] + _rms(moe, gpost_ref[...])


def _combine(dest, y, route, h, gpost):
    n, d = h.shape
    tc = TC_COMB
    grid_spec = pltpu.PrefetchScalarGridSpec(
        num_scalar_prefetch=1,
        grid=(n // tc,),
        in_specs=[pl.BlockSpec(memory_space=pl.ANY),
                  pl.BlockSpec((tc, LANES), lambda i, ds: (i, 0)),
                  pl.BlockSpec((tc, d), lambda i, ds: (i, 0)),
                  pl.BlockSpec((1, d), lambda i, ds: (0, 0))],
        out_specs=pl.BlockSpec((tc, d), lambda i, ds: (i, 0)),
        scratch_shapes=[pltpu.VMEM((2, tc, d), F32), pltpu.VMEM((2, tc, d), F32),
                        pltpu.SemaphoreType.DMA((2, 2))],
    )
    return pl.pallas_call(
        _combine_kernel,
        grid_spec=grid_spec,
        out_shape=jax.ShapeDtypeStruct((n, d), F32),
        compiler_params=_params(("arbitrary",)),
        name="moe_combine",
    )(dest, y, route, h, gpost)


def _overlap_table_t(seq):
    nc = (seq - CMP_LEN) // CMP_STRIDE + 1
    ncp = seq // CMP_STRIDE
    nsb = seq // SEL_LEN
    cs = np.arange(ncp)[None, :] * CMP_STRIDE
    js = np.arange(LANES)[:, None] * SEL_LEN
    ov = (cs < js + SEL_LEN) & (cs + CMP_LEN > js) & (np.arange(ncp)[None, :] < nc) & (np.arange(LANES)[:, None] < nsb)
    return jnp.asarray(ov.astype(np.float32), BF16)


def _expand_table_t(seq):
    e = (np.arange(seq)[:, None] // SEL_LEN) == np.arange(LANES)[None, :]
    return jnp.asarray(e.astype(np.float32), BF16)


def _layer0_weights(w_in):
    d = w_in.shape[0]
    qc = N_HEADS * HEAD_DIM
    kvc = 6 * N_KV * HEAD_DIM
    gc = 3 * N_HEADS
    gh = N_KV * HEAD_DIM
    wq = (w_in[:, :qc] * (HEAD_DIM ** -0.5 * np.log2(np.e))).reshape(d, N_KV, HPG, HEAD_DIM)
    zeros = jnp.zeros_like(wq[:, 0])
    wq_pad = jnp.stack([jnp.concatenate([wq[:, 0], zeros], axis=-1),
                        jnp.concatenate([zeros, wq[:, 1]], axis=-1)], axis=1)
    wqt = wq_pad.reshape(d, N_HEADS * LANES).T.astype(BF16)
    wkv = w_in[:, qc:qc + kvc]
    wvt = jnp.concatenate([wkv[:, 3 * gh:4 * gh], wkv[:, 5 * gh:6 * gh]], axis=1).T.astype(BF16)
    wg = w_in[:, qc + kvc:qc + kvc + gc].reshape(d, N_KV, 3 * HPG)
    wgt = jnp.pad(wg, ((0, 0), (0, 0), (0, LANES - 3 * HPG))).reshape(d, N_KV * LANES).T.astype(BF16)
    wglu = w_in[:, qc + kvc + gc:].astype(BF16)
    return wqt, wkv.astype(BF16), wvt, wgt, wglu


def _dispatch_plan(route, n_tok):
    tm = TM_MOE
    e_flat = route[:, 0:2].astype(jnp.int32).reshape(-1)
    onehot = (e_flat[:, None] == jnp.arange(N_EXPERTS, dtype=jnp.int32)[None, :]).astype(jnp.int32)
    rank = jnp.sum((jnp.cumsum(onehot, axis=0) - onehot) * onehot, axis=1)
    counts = jnp.sum(onehot, axis=0)
    padded = (counts + tm - 1) // tm * tm
    pad_ends = jnp.cumsum(padded)
    pad_starts = pad_ends - padded
    dest = (pad_starts[e_flat] + rank).astype(jnp.int32)
    n_rows = 2 * n_tok + N_EXPERTS * tm
    n_tiles = n_rows // tm
    rowtok = jnp.zeros((n_rows,), jnp.int32).at[dest].set(jnp.arange(2 * n_tok, dtype=jnp.int32) // 2)
    nused = (pad_ends[-1] // tm).astype(jnp.int32)
    tiles = jnp.minimum(jnp.arange(n_tiles, dtype=jnp.int32), nused - 1) * tm
    tile_e = jnp.minimum(jnp.sum((pad_ends[None, :] <= tiles[:, None]).astype(jnp.int32), axis=1), N_EXPERTS - 1)
    return tile_e, nused.reshape(1), rowtok, dest


def kernel(x, norm_mix_pre, norm_mix_post, norm_ffn_pre, norm_ffn_post, nsa_conv_w_in, cmp_pe, cmp_w1, cmp_b1, cmp_w2, cmp_b2, conv_w, conv_b, conv_ln_g, conv_ln_b, nsa_conv_w_out, ffn_w_gate, ffn_w_up, ffn_w_down, gmlp_w_in, gmlp_ln_g, gmlp_ln_b, gmlp_w_s, gmlp_b_s, gmlp_w_out, moe_w_router, moe_w_gate, moe_w_up, moe_w_down):
    b, s, d = x.shape
    n = b * s
    assert d == D_MODEL and s // SEL_LEN == SEL_LEN and s % (2 * TK_SEL) == 0 and n % TM_PROJ == 0
    h0 = x.reshape(n, d)
    vec = lambda a: a.reshape(1, -1).astype(F32)

    wqt, wkv, wvt, wgt, wglu = _layer0_weights(nsa_conv_w_in[0])
    qt, kv, vt, gt, c = _proj0(h0, vec(norm_mix_pre[0]), wqt, wkv, wvt, wgt, wglu, b)
    kv3 = kv.reshape(b, s, 6 * N_KV * HEAD_DIM)

    nr = s // CMP_STRIDE
    kr = kv3[:, :, :2 * N_KV * HEAD_DIM].reshape(b, nr, CMP_STRIDE, 2, N_KV, HEAD_DIM)
    kr = kr.transpose(3, 0, 4, 1, 2, 5).reshape(2, b * N_KV, nr, CMP_STRIDE * HEAD_DIM)
    cmp = _compress(kr, cmp_pe[0].reshape(2, 1, CMP_LEN * HEAD_DIM).astype(F32), cmp_w1[0].astype(BF16),
                    cmp_b1[0].reshape(2, 1, CMP_HIDDEN).astype(F32), cmp_w2[0].astype(BF16),
                    cmp_b2[0].reshape(2, 1, HEAD_DIM).astype(F32))
    cmp = cmp.reshape(2, b, N_KV, nr, HEAD_DIM)
    kc = cmp[0].transpose(0, 2, 1, 3).reshape(b, nr, N_KV * HEAD_DIM)
    vct = cmp[1].transpose(0, 1, 3, 2).reshape(b, N_KV * HEAD_DIM, nr)

    o_attn = _attention(qt, kc, vct, kv3, vt, gt, _overlap_table_t(s), _expand_table_t(s))
    cc = _conv_branch(c.reshape(b, s, CONV_CH), conv_w[0].reshape(CONV_WIDTH, CONV_CH).astype(F32),
                      vec(conv_b[0]), vec(conv_ln_g[0]), vec(conv_ln_b[0]))
    w_out = nsa_conv_w_out[0].astype(BF16)
    nw = N_HEADS * HEAD_DIM
    h1, u1 = _outproj(o_attn.reshape(n, nw), cc.reshape(n, CONV_CH), w_out[:nw], w_out[nw:], h0,
                      vec(norm_mix_post[0]), vec(norm_ffn_pre[0]))
    h2, u2 = _ffn(u1, ffn_w_gate[0].astype(BF16), ffn_w_up[0].astype(BF16), ffn_w_down[0].astype(BF16), h1,
                  vec(norm_ffn_post[0]), vec(norm_mix_pre[1]))

    bsb = jnp.repeat(gmlp_b_s[0].T.astype(F32), d // GMLP_GROUPS, axis=1)
    wr = jnp.pad(moe_w_router[0].astype(F32), ((0, 0), (0, LANES - N_EXPERTS)))
    wr_hi = wr.astype(BF16)
    wr_lo = (wr - wr_hi.astype(F32)).astype(BF16)
    h3, u4, route = _gmlp(u2, gmlp_w_in[0].astype(BF16), vec(gmlp_ln_g[0]), vec(gmlp_ln_b[0]),
                          gmlp_w_s[0].astype(F32), bsb, gmlp_w_out[0].astype(BF16), h2,
                          vec(norm_mix_post[1]), vec(norm_ffn_pre[1]), wr_hi, wr_lo)
    tile_e, nused, rowtok, dest = _dispatch_plan(route, n)
    y = _experts(tile_e, nused, rowtok, u4, moe_w_gate[0].astype(BF16), moe_w_up[0].astype(BF16),
                 moe_w_down[0].astype(BF16))
    h4 = _combine(dest, y, route, h3, vec(norm_ffn_post[1]))
    return h4.reshape(b, s, d)
```

```python
import functools

import numpy as np
import jax
import jax.numpy as jnp
from jax import lax
from jax.experimental import pallas as pl
from jax.experimental.pallas import tpu as pltpu

F32 = jnp.float32
BF16 = jnp.bfloat16

D_MODEL = 1024
N_HEADS = 8
N_KV = 2
HPG = N_HEADS // N_KV
HEAD_DIM = 64
CMP_LEN = 32
CMP_STRIDE = 16
CMP_HIDDEN = 256
SEL_LEN = 64
N_SEL = 16
WINDOW = 512
CONV_CH = D_MODEL // 2
CONV_WIDTH = 31
GMLP_GROUPS = 8
GMLP_CHUNK = 128
N_EXPERTS = 8
EPS = 1e-6
NEG = -1e30

LANES = 128
TQ = 128
TK_SEL = 1024
TM_PROJ = 512
TS_CONV = 512
CONV_HALO = 32
TM_FFN = 512
TF_FFN = 1408
TM_MOE = 512
TF_MOE = 1792
TC_COMB = 256
VMEM_LIMIT = 56 * 1024 * 1024


def _params(sem):
    return pltpu.CompilerParams(dimension_semantics=sem, vmem_limit_bytes=VMEM_LIMIT)


def _dot(a, b):
    return jnp.dot(a, b, preferred_element_type=F32)


def _dot_nt(a, b):
    return lax.dot_general(a, b, (((1,), (1,)), ((), ())), preferred_element_type=F32)


def _rms(x, g):
    return x * lax.rsqrt(jnp.mean(x * x, axis=-1, keepdims=True) + EPS) * g


def _layer_norm(x, g, b):
    mu = jnp.mean(x, axis=-1, keepdims=True)
    xc = x - mu
    var = jnp.mean(xc * xc, axis=-1, keepdims=True)
    return xc * lax.rsqrt(var + EPS) * g + b


def _sigmoid(x):
    return 1.0 / (1.0 + jnp.exp(-x))


def _silu(x):
    return x * _sigmoid(x)


def _gelu_tanh(x):
    c = np.float32(np.sqrt(2.0 / np.pi))
    return 0.5 * x * (1.0 + jnp.tanh(c * (x + 0.044715 * (x * x * x))))


def _proj0_kernel(h_ref, g_ref, wqt_ref, wkv_ref, wvt_ref, wgt_ref, wglu_ref,
                  qt_ref, kv_ref, vt_ref, gt_ref, c_ref):
    u = _rms(h_ref[...], g_ref[...]).astype(BF16)
    qt_ref[0] = _dot_nt(wqt_ref[...], u).astype(BF16)
    kv_ref[...] = _dot(u, wkv_ref[...]).astype(BF16)
    vt_ref[0] = _dot_nt(wvt_ref[...], u).astype(BF16)
    gt_ref[0] = _sigmoid(_dot_nt(wgt_ref[...], u))
    glu = _dot(u, wglu_ref[...])
    c_ref[...] = glu[:, :CONV_CH] * _sigmoid(glu[:, CONV_CH:])


def _proj0(h, g, wqt, wkv, wvt, wgt, wglu, batch):
    n, d = h.shape
    tm = TM_PROJ
    seq = n // batch
    per = seq // tm
    full = lambda a: pl.BlockSpec(a.shape, lambda i: (0,) * a.ndim)
    row = lambda c: pl.BlockSpec((tm, c), lambda i: (i, 0))
    col = lambda r: pl.BlockSpec((1, r, tm), lambda i: (i // per, 0, i % per))
    return pl.pallas_call(
        _proj0_kernel,
        grid=(n // tm,),
        in_specs=[row(d), full(g), full(wqt), full(wkv), full(wvt), full(wgt), full(wglu)],
        out_specs=[col(wqt.shape[0]), row(wkv.shape[1]), col(wvt.shape[0]), col(wgt.shape[0]), row(CONV_CH)],
        out_shape=[jax.ShapeDtypeStruct((batch, wqt.shape[0], seq), BF16),
                   jax.ShapeDtypeStruct((n, wkv.shape[1]), BF16),
                   jax.ShapeDtypeStruct((batch, wvt.shape[0], seq), BF16),
                   jax.ShapeDtypeStruct((batch, wgt.shape[0], seq), F32),
                   jax.ShapeDtypeStruct((n, CONV_CH), F32)],
        compiler_params=_params(("parallel",)),
        name="proj0",
    )(h, g, wqt, wkv, wvt, wgt, wglu)


def _compress_kernel(kr_ref, pe_ref, w1_ref, b1_ref, w2_ref, b2_ref, o_ref):
    half = CMP_STRIDE * HEAD_DIM
    kr = kr_ref[0, 0].astype(F32)
    pe = pe_ref[0]
    top = (kr + pe[:, :half]).astype(BF16)
    bot = (kr + pe[:, half:]).astype(BF16)
    a = _dot(top, w1_ref[0, :half, :])
    b = _dot(bot, w1_ref[0, half:, :])
    nrow = b.shape[0]
    pre = a + pltpu.roll(b, nrow - 1, 0) + b1_ref[0]
    hid = _gelu_tanh(pre).astype(BF16)
    o_ref[0, 0] = (_dot(hid, w2_ref[0]) + b2_ref[0]).astype(o_ref.dtype)


def _compress(kr, pe, w1, b1, w2, b2):
    _, bg, nr, feat = kr.shape
    return pl.pallas_call(
        _compress_kernel,
        grid=(2, bg),
        in_specs=[pl.BlockSpec((1, 1, nr, feat), lambda j, i: (j, i, 0, 0)),
                  pl.BlockSpec((1, 1, 2 * feat), lambda j, i: (j, 0, 0)),
                  pl.BlockSpec((1, 2 * feat, CMP_HIDDEN), lambda j, i: (j, 0, 0)),
                  pl.BlockSpec((1, 1, CMP_HIDDEN), lambda j, i: (j, 0, 0)),
                  pl.BlockSpec((1, CMP_HIDDEN, HEAD_DIM), lambda j, i: (j, 0, 0)),
                  pl.BlockSpec((1, 1, HEAD_DIM), lambda j, i: (j, 0, 0))],
        out_specs=pl.BlockSpec((1, 1, nr, HEAD_DIM), lambda j, i: (j, i, 0, 0)),
        out_shape=jax.ShapeDtypeStruct((2, bg, nr, HEAD_DIM), BF16),
        compiler_params=_params(("parallel", "parallel")),
        name="compress",
    )(kr, pe, w1, b1, w2, b2)


def _col_reduce(x, op):
    rows = x.shape[0]
    part = op(x.reshape(4, rows // 4, x.shape[1]), axis=0)
    return op(part, axis=0, keepdims=True)


def _attn_kernel(qt_ref, kc_ref, vct_ref, ks_ref, vst_ref, kw_ref, vwt_ref, gt_ref,
                 ovt_ref, et_ref, o_ref, rhs_ref, m_ref, l_ref, acc_ref):
    qi = pl.program_id(1)
    q0 = qi * TQ
    nsb = SEL_LEN
    ncp = kc_ref.shape[1]
    wkeys = WINDOW + TQ
    groups = range(N_KV)
    tile4 = lambda a: jnp.concatenate([a] * HPG, axis=1)
    qgs = [jnp.concatenate(
        [qt_ref[0, LANES * (HPG * g + h):LANES * (HPG * g + h + 1), :] for h in range(HPG)], axis=1)
        for g in groups]

    n_idx = lax.broadcasted_iota(jnp.int32, (ncp, TQ), 0)
    t_idx = q0 + lax.broadcasted_iota(jnp.int32, (ncp, TQ), 1)
    cmask = tile4(n_idx * CMP_STRIDE + (CMP_LEN - 1) <= t_idx)
    o_c, imp_t = [], []
    for g in groups:
        s = jnp.where(cmask, _dot(kc_ref[0], qgs[g]), NEG)
        m = _col_reduce(s, jnp.max)
        p = jnp.where(cmask, jnp.exp2(s - m), 0.0)
        l = _col_reduce(p, jnp.sum)
        p = p / jnp.where(l > 0.0, l, 1.0)
        o_c.append(_dot(vct_ref[0], p.astype(BF16)))
        psum = p[:, 0:TQ] + p[:, TQ:2 * TQ] + p[:, 2 * TQ:3 * TQ] + p[:, 3 * TQ:4 * TQ]
        imp_t.append(_dot(ovt_ref[...], psum.astype(BF16))[:nsb])

    j_idx = lax.broadcasted_iota(jnp.int32, (nsb, TQ), 0)
    cur = (q0 + lax.broadcasted_iota(jnp.int32, (nsb, TQ), 1)) // SEL_LEN
    valid = j_idx <= cur
    forced = (j_idx == 0) | (j_idx == cur) | (j_idx == cur - 1)
    sub = lax.broadcasted_iota(jnp.int32, (8, TQ), 0)
    nch = nsb // 8
    for g in groups:
        score = jnp.where(valid, imp_t[g] + jnp.where(forced, 1e6, 0.0), -1e9)
        chunks = [score[8 * v:8 * v + 8] for v in range(nch)]
        counts = [jnp.zeros((8, TQ), F32) for _ in range(nch)]
        for i in range(nsb):
            row = jnp.broadcast_to(score[i:i + 1], (8, TQ))
            for v in range(nch):
                if i < 8 * v:
                    ahead = row >= chunks[v]
                elif i >= 8 * v + 8:
                    ahead = row > chunks[v]
                else:
                    ahead = jnp.where(sub > (i - 8 * v),
                                      jnp.where(row >= chunks[v], 1.0, 0.0),
                                      jnp.where(row > chunks[v], 1.0, 0.0)) > 0.5
                counts[v] = counts[v] + jnp.where(ahead, 1.0, 0.0)
        bias_t = jnp.concatenate(
            [jnp.where((c < float(N_SEL)) & valid[8 * v:8 * v + 8], 0.0, NEG) for v, c in enumerate(counts)]
            + [jnp.zeros((LANES - nsb, TQ), F32)], axis=0).astype(BF16)
        rhs_ref[g] = jnp.concatenate([qgs[g], tile4(bias_t)], axis=0)

    m_ref[...] = jnp.full(m_ref.shape, NEG, F32)
    l_ref[...] = jnp.zeros(l_ref.shape, F32)
    acc_ref[...] = jnp.zeros(acc_ref.shape, F32)

    def sel_tile(kt, diagonal):
        k0 = pl.multiple_of(kt * TK_SEL, TK_SEL)
        lhs = jnp.concatenate([ks_ref[0, pl.ds(k0, TK_SEL), :], et_ref[pl.ds(k0, TK_SEL), :]], axis=1)
        vt = vst_ref[0, :, pl.ds(k0, TK_SEL)]
        if diagonal:
            kpos = k0 + lax.broadcasted_iota(jnp.int32, (TK_SEL, TQ), 0)
            tpos = q0 + lax.broadcasted_iota(jnp.int32, (TK_SEL, TQ), 1)
            causal = tile4(kpos <= tpos)
        for g in groups:
            sc = _dot(lhs, rhs_ref[g])
            if diagonal:
                sc = jnp.where(causal, sc, NEG)
            m_i = m_ref[g]
            m_new = jnp.maximum(m_i, _col_reduce(sc, jnp.max))
            alpha = jnp.exp2(m_i - m_new)
            pexp = jnp.exp2(sc - m_new)
            l_ref[g] = alpha * l_ref[g] + _col_reduce(pexp, jnp.sum)
            acc_ref[g] = alpha * acc_ref[g] + _dot(vt, pexp.astype(BF16))
            m_ref[g] = m_new

    last = q0 // TK_SEL

    def sel_body(kt, carry):
        sel_tile(kt, False)
        return carry

    lax.fori_loop(0, last, sel_body, 0)
    sel_tile(last, True)

    k0 = pl.multiple_of(jnp.maximum(q0 - WINDOW, 0), TQ)
    kpos = k0 + lax.broadcasted_iota(jnp.int32, (wkeys, TQ), 0)
    tpos = q0 + lax.broadcasted_iota(jnp.int32, (wkeys, TQ), 1)
    wmask = tile4((kpos <= tpos) & (kpos > tpos - WINDOW))
    kwin = kw_ref[0, pl.ds(k0, wkeys), :]
    vwin = vwt_ref[0, :, pl.ds(k0, wkeys)]
    o_w = []
    for g in groups:
        sc = jnp.where(wmask, _dot(kwin, qgs[g]), NEG)
        pexp = jnp.exp2(sc - _col_reduce(sc, jnp.max))
        o_w.append(_dot(vwin, pexp.astype(BF16)) / _col_reduce(pexp, jnp.sum))

    blocks = []
    for g in groups:
        o_s = acc_ref[g] / l_ref[g]
        for h in range(HPG):
            r0 = LANES * g + 3 * h
            cols = slice(h * TQ, (h + 1) * TQ)
            og = (gt_ref[0, r0:r0 + 1, :] * o_c[g][:, cols] + gt_ref[0, r0 + 1:r0 + 2, :] * o_s[:, cols]
                  + gt_ref[0, r0 + 2:r0 + 3, :] * o_w[g][:, cols])
            blocks.append(og[HEAD_DIM * g:HEAD_DIM * (g + 1)])
    o_ref[0] = jnp.concatenate(blocks, axis=0).T.astype(o_ref.dtype)


def _attention(qt, kc, vct, kv, vt, gt, ovt, et):
    b, _, s = qt.shape
    ncp = kc.shape[1]
    kspec = lambda j: pl.BlockSpec((1, s, LANES), lambda bi, qi: (bi, 0, j))
    vspec = lambda j: pl.BlockSpec((1, LANES, s), lambda bi, qi: (bi, j, 0))
    return pl.pallas_call(
        _attn_kernel,
        grid=(b, s // TQ),
        in_specs=[pl.BlockSpec((1, N_HEADS * LANES, TQ), lambda bi, qi: (bi, 0, qi)),
                  pl.BlockSpec((1, ncp, LANES), lambda bi, qi: (bi, 0, 0)),
                  pl.BlockSpec((1, LANES, ncp), lambda bi, qi: (bi, 0, 0)),
                  kspec(2), vspec(0), kspec(4), vspec(1),
                  pl.BlockSpec((1, N_KV * LANES, TQ), lambda bi, qi: (bi, 0, qi)),
                  pl.BlockSpec(ovt.shape, lambda bi, qi: (0, 0)),
                  pl.BlockSpec(et.shape, lambda bi, qi: (0, 0))],
        out_specs=pl.BlockSpec((1, TQ, N_HEADS * HEAD_DIM), lambda bi, qi: (bi, qi, 0)),
        out_shape=jax.ShapeDtypeStruct((b, s, N_HEADS * HEAD_DIM), BF16),
        scratch_shapes=[pltpu.VMEM((N_KV, 2 * LANES, HPG * TQ), BF16),
                        pltpu.VMEM((N_KV, 1, HPG * TQ), F32), pltpu.VMEM((N_KV, 1, HPG * TQ), F32),
                        pltpu.VMEM((N_KV, LANES, HPG * TQ), F32)],
        compiler_params=_params(("parallel", "arbitrary")),
        name="nsa_attention",
    )(qt, kc, vct, kv, vt, kv, vt, gt, ovt, et)


def _conv_kernel(c_ref, halo_ref, cw_ref, cb_ref, g_ref, b_ref, o_ref, xs_ref):
    i = pl.program_id(1)
    ts = c_ref.shape[1]
    xs_ref[0:CONV_HALO, :] = jnp.where(i > 0, halo_ref[0], 0.0)
    xs_ref[CONV_HALO:CONV_HALO + ts, :] = c_ref[0]
    acc = jnp.zeros((ts, CONV_CH), F32)
    for w in range(CONV_WIDTH):
        off = CONV_HALO - (CONV_WIDTH - 1) + w
        acc = acc + xs_ref[off:off + ts, :] * cw_ref[w:w + 1, :]
    y = _layer_norm(acc + cb_ref[...], g_ref[...], b_ref[...])
    o_ref[0] = _silu(y).astype(o_ref.dtype)


def _conv_branch(c, cw, cb, ln_g, ln_b):
    b, s, ch = c.shape
    ts = TS_CONV
    per = ts // CONV_HALO
    vec = pl.BlockSpec((1, ch), lambda bi, i: (0, 0))
    return pl.pallas_call(
        _conv_kernel,
        grid=(b, s // ts),
        in_specs=[pl.BlockSpec((1, ts, ch), lambda bi, i: (bi, i, 0)),
                  pl.BlockSpec((1, CONV_HALO, ch), lambda bi, i: (bi, jnp.maximum(i * per - 1, 0), 0)),
                  pl.BlockSpec((CONV_WIDTH, ch), lambda bi, i: (0, 0)), vec, vec, vec],
        out_specs=pl.BlockSpec((1, ts, ch), lambda bi, i: (bi, i, 0)),
        out_shape=jax.ShapeDtypeStruct((b, s, ch), BF16),
        scratch_shapes=[pltpu.VMEM((CONV_HALO + ts, ch), F32)],
        compiler_params=_params(("parallel", "arbitrary")),
        name="conv_branch",
    )(c, c, cw, cb, ln_g, ln_b)


def _outproj_kernel(oa_ref, cc_ref, wt_ref, wb_ref, h_ref, gpost_ref, gnext_ref, h1_ref, u_ref):
    m = _dot(oa_ref[...], wt_ref[...]) + _dot(cc_ref[...], wb_ref[...])
    h1 = h_ref[...] + _rms(m, gpost_ref[...])
    h1_ref[...] = h1
    u_ref[...] = _rms(h1, gnext_ref[...]).astype(u_ref.dtype)


def _outproj(oa, cc, wt, wb, h, gpost, gnext):
    n, d = h.shape
    tm = TM_PROJ
    full = lambda a: pl.BlockSpec(a.shape, lambda i: (0,) * a.ndim)
    row = lambda c: pl.BlockSpec((tm, c), lambda i: (i, 0))
    return pl.pallas_call(
        _outproj_kernel,
        grid=(n // tm,),
        in_specs=[row(oa.shape[1]), row(cc.shape[1]), full(wt), full(wb), row(d), full(gpost), full(gnext)],
        out_specs=[row(d), row(d)],
        out_shape=[jax.ShapeDtypeStruct((n, d), F32), jax.ShapeDtypeStruct((n, d), BF16)],
        compiler_params=_params(("parallel",)),
        name="outproj",
    )(oa, cc, wt, wb, h, gpost, gnext)


def _ffn_kernel(u_ref, wg_ref, wu_ref, wd_ref, h_ref, gpost_ref, gnext_ref, h2_ref, u2_ref, acc_ref):
    f = pl.program_id(1)

    @pl.when(f == 0)
    def _():
        acc_ref[...] = jnp.zeros_like(acc_ref)

    u = u_ref[...]
    act = (_silu(_dot(u, wg_ref[...])) * _dot(u, wu_ref[...])).astype(BF16)
    acc_ref[...] += _dot(act, wd_ref[...])

    @pl.when(f == pl.num_programs(1) - 1)
    def _():
        h2 = h_ref[...] + _rms(acc_ref[...], gpost_ref[...])
        h2_ref[...] = h2
        u2_ref[...] = _rms(h2, gnext_ref[...]).astype(u2_ref.dtype)


def _ffn(u, wg, wu, wd, h, gpost, gnext):
    n, d = h.shape
    ff = wg.shape[1]
    tm, tf = TM_FFN, TF_FFN
    row = pl.BlockSpec((tm, d), lambda i, f: (i, 0))
    vec = pl.BlockSpec((1, d), lambda i, f: (0, 0))
    return pl.pallas_call(
        _ffn_kernel,
        grid=(n // tm, ff // tf),
        in_specs=[row, pl.BlockSpec((d, tf), lambda i, f: (0, f)), pl.BlockSpec((d, tf), lambda i, f: (0, f)),
                  pl.BlockSpec((tf, d), lambda i, f: (f, 0)), row, vec, vec],
        out_specs=[row, row],
        out_shape=[jax.ShapeDtypeStruct((n, d), F32), jax.ShapeDtypeStruct((n, d), BF16)],
        scratch_shapes=[pltpu.VMEM((tm, d), F32)],
        compiler_params=_params(("parallel", "arbitrary")),
        name="dense_ffn",
    )(u, wg, wu, wd, h, gpost, gnext)


def _gmlp_kernel(u_ref, win_ref, lng_ref, lnb_ref, ws_ref, bs_ref, wout_ref, h_ref, gpost_ref, gnext_ref,
                 wrh_ref, wrl_ref, h3_ref, u4_ref, route_ref, mix_ref):
    tm = u_ref.shape[0]
    width = wout_ref.shape[0]
    z = _gelu_tanh(_dot(u_ref[...], win_ref[...]))
    z1 = z[:, :width]
    z2 = _layer_norm(z[:, width:], lng_ref[...], lnb_ref[...]).astype(BF16)
    r_idx = lax.broadcasted_iota(jnp.int32, (GMLP_CHUNK, GMLP_CHUNK), 0)
    c_idx = lax.broadcasted_iota(jnp.int32, (GMLP_CHUNK, GMLP_CHUNK), 1)
    gw = width // GMLP_GROUPS
    for g in range(GMLP_GROUPS):
        wsg = jnp.where(c_idx <= r_idx, ws_ref[g], 0.0).astype(BF16)
        for c in range(tm // GMLP_CHUNK):
            rows = slice(c * GMLP_CHUNK, (c + 1) * GMLP_CHUNK)
            cols = slice(g * gw, (g + 1) * gw)
            mix_ref[rows, cols] = _dot(wsg, z2[rows, cols]) + bs_ref[:, cols]
    gated = (z1 * mix_ref[...]).astype(BF16)
    y = _dot(gated, wout_ref[...])
    h3 = h_ref[...] + _rms(y, gpost_ref[...])
    h3_ref[...] = h3
    u4 = _rms(h3, gnext_ref[...])
    u4_ref[...] = u4

    u_hi = u4.astype(BF16)
    u_lo = (u4 - u_hi.astype(F32)).astype(BF16)
    logits = _dot(u_hi, wrh_ref[...]) + (_dot(u_lo, wrh_ref[...]) + _dot(u_hi, wrl_ref[...]))
    lane = lax.broadcasted_iota(jnp.int32, (tm, LANES), 1).astype(F32)
    lg = jnp.where(lane < float(N_EXPERTS), logits, NEG)
    m1 = jnp.max(lg, axis=-1, keepdims=True)
    i1 = jnp.min(jnp.where(lg == m1, lane, float(LANES)), axis=-1, keepdims=True)
    lg2 = jnp.where(lane == i1, NEG, lg)
    m2 = jnp.max(lg2, axis=-1, keepdims=True)
    i2 = jnp.min(jnp.where(lg2 == m2, lane, float(LANES)), axis=-1, keepdims=True)
    e2 = jnp.exp(m2 - m1)
    den = 1.0 + e2
    route_ref[...] = jnp.where(lane == 0.0, i1, jnp.where(lane == 1.0, i2,
                               jnp.where(lane == 2.0, 1.0 / den, jnp.where(lane == 3.0, e2 / den, 0.0))))


def _gmlp(u, win, lng, lnb, ws, bsb, wout, h, gpost, gnext, wrh, wrl):
    n, d = h.shape
    tm = TM_PROJ
    full = lambda a: pl.BlockSpec(a.shape, lambda i: (0,) * a.ndim)
    row = lambda c: pl.BlockSpec((tm, c), lambda i: (i, 0))
    return pl.pallas_call(
        _gmlp_kernel,
        grid=(n // tm,),
        in_specs=[row(d), full(win), full(lng), full(lnb), full(ws), full(bsb), full(wout), row(d),
                  full(gpost), full(gnext), full(wrh), full(wrl)],
        out_specs=[row(d), row(d), row(LANES)],
        out_shape=[jax.ShapeDtypeStruct((n, d), F32), jax.ShapeDtypeStruct((n, d), F32),
                   jax.ShapeDtypeStruct((n, LANES), F32)],
        scratch_shapes=[pltpu.VMEM((tm, wout.shape[0]), F32)],
        compiler_params=_params(("parallel",)),
        name="gmlp_router",
    )(u, win, lng, lnb, ws, bsb, wout, h, gpost, gnext, wrh, wrl)


def _expert_kernel(tile_e_ref, nused_ref, rowtok_ref, x_hbm, wg_ref, wu_ref, wd_ref, y_ref,
                   xg_ref, xb_ref, acc_ref, sem, *, nf):
    i = pl.program_id(0)
    f = pl.program_id(1)
    tm = xg_ref.shape[1]
    nused = nused_ref[0]
    active = i < nused
    slot = i % 2

    def row_copy(tok, s, r):
        return pltpu.make_async_copy(x_hbm.at[pl.ds(tok, 1)], xg_ref.at[s, pl.ds(r, 1)], sem.at[s])

    def wait_tile(s):
        pltpu.make_async_copy(x_hbm.at[pl.ds(0, tm)], xg_ref.at[s], sem.at[s]).wait()

    @pl.when((i == 0) & (f == 0))
    def _():
        def body(r, carry):
            row_copy(rowtok_ref[r], 0, r).start()
            return carry

        lax.fori_loop(0, tm, body, 0, unroll=8)

    @pl.when((i <= nused) & (f == 0))
    def _():
        wait_tile(slot)

    @pl.when(active & (f == 0))
    def _():
        xb_ref[...] = xg_ref[slot].astype(BF16)
        acc_ref[...] = jnp.zeros_like(acc_ref)

    @pl.when(active)
    def _():
        per = tm // nf
        base = (i + 1) * tm + f * per
        for j in range(per):
            row_copy(rowtok_ref[base + j], 1 - slot, f * per + j).start()
        x = xb_ref[...]
        act = (_silu(_dot(x, wg_ref[0])) * _dot(x, wu_ref[0])).astype(BF16)
        acc_ref[...] += _dot(act, wd_ref[0])

    last = f == pl.num_programs(1) - 1

    @pl.when(active & last)
    def _():
        y_ref[...] = acc_ref[...]

    @pl.when(jnp.logical_not(active) & last)
    def _():
        y_ref[...] = jnp.zeros_like(y_ref)


def _experts(tile_e, nused, rowtok, x, wg, wu, wd):
    n_rows = rowtok.shape[0]
    d = x.shape[1]
    ff = wg.shape[2]
    tm, tf = TM_MOE, TF_MOE
    nf = ff // tf

    def fidx(i, f, te, nu):
        return jnp.where(i < nu[0], f, nf - 1)

    grid_spec = pltpu.PrefetchScalarGridSpec(
        num_scalar_prefetch=3,
        grid=(n_rows // tm, nf),
        in_specs=[pl.BlockSpec(memory_space=pl.ANY),
                  pl.BlockSpec((1, d, tf), lambda i, f, te, nu, rt: (te[i], 0, fidx(i, f, te, nu))),
                  pl.BlockSpec((1, d, tf), lambda i, f, te, nu, rt: (te[i], 0, fidx(i, f, te, nu))),
                  pl.BlockSpec((1, tf, d), lambda i, f, te, nu, rt: (te[i], fidx(i, f, te, nu), 0))],
        out_specs=pl.BlockSpec((tm, d), lambda i, f, te, nu, rt: (i, 0)),
        scratch_shapes=[pltpu.VMEM((2, tm, d), F32), pltpu.VMEM((tm, d), BF16), pltpu.VMEM((tm, d), F32),
                        pltpu.SemaphoreType.DMA((2,))],
    )
    return pl.pallas_call(
        functools.partial(_expert_kernel, nf=nf),
        grid_spec=grid_spec,
        out_shape=jax.ShapeDtypeStruct((n_rows, d), F32),
        compiler_params=_params(("arbitrary", "arbitrary")),
        name="moe_experts",
    )(tile_e, nused, rowtok, x, wg, wu, wd)


def _combine_kernel(dest_ref, y_hbm, route_ref, h_ref, gpost_ref, o_ref, ya_ref, yb_ref, sem):
    i = pl.program_id(0)
    tc = ya_ref.shape[1]
    slot = i % 2

    def copies(d0, d1, s, r):
        return (pltpu.make_async_copy(y_hbm.at[pl.ds(d0, 1)], ya_ref.at[s, pl.ds(r, 1)], sem.at[0, s]),
                pltpu.make_async_copy(y_hbm.at[pl.ds(d1, 1)], yb_ref.at[s, pl.ds(r, 1)], sem.at[1, s]))

    def issue_tile(t, s):
        for r in range(tc):
            p = 2 * (t * tc + r)
            ca, cb = copies(dest_ref[p], dest_ref[p + 1], s, r)
            ca.start()
            cb.start()

    def wait_tile(s):
        pltpu.make_async_copy(y_hbm.at[pl.ds(0, tc)], ya_ref.at[s], sem.at[0, s]).wait()
        pltpu.make_async_copy(y_hbm.at[pl.ds(0, tc)], yb_ref.at[s], sem.at[1, s]).wait()

    @pl.when(i == 0)
    def _():
        issue_tile(0, 0)

    @pl.when(i + 1 < pl.num_programs(0))
    def _():
        issue_tile(i + 1, 1 - slot)

    wait_tile(slot)
    route = route_ref[...]
    moe = route[:, 2:3] * ya_ref[slot] + route[:, 3:4] * yb_ref[slot]
    o_ref[...] = h_ref[...] + _rms(moe, gpost_ref[...])


def _combine(dest, y, route, h, gpost):
    n, d = h.shape
    tc = TC_COMB
    grid_spec = pltpu.PrefetchScalarGridSpec(
        num_scalar_prefetch=1,
        grid=(n // tc,),
        in_specs=[pl.BlockSpec(memory_space=pl.ANY),
                  pl.BlockSpec((tc, LANES), lambda i, ds: (i, 0)),
                  pl.BlockSpec((tc, d), lambda i, ds: (i, 0)),
                  pl.BlockSpec((1, d), lambda i, ds: (0, 0))],
        out_specs=pl.BlockSpec((tc, d), lambda i, ds: (i, 0)),
        scratch_shapes=[pltpu.VMEM((2, tc, d), F32), pltpu.VMEM((2, tc, d), F32),
                        pltpu.SemaphoreType.DMA((2, 2))],
    )
    return pl.pallas_call(
        _combine_kernel,
        grid_spec=grid_spec,
        out_shape=jax.ShapeDtypeStruct((n, d), F32),
        compiler_params=_params(("arbitrary",)),
        name="moe_combine",
    )(dest, y, route, h, gpost)


def _overlap_table_t(seq):
    nc = (seq - CMP_LEN) // CMP_STRIDE + 1
    ncp = seq // CMP_STRIDE
    nsb = seq // SEL_LEN
    cs = np.arange(ncp)[None, :] * CMP_STRIDE
    js = np.arange(LANES)[:, None] * SEL_LEN
    ov = (cs < js + SEL_LEN) & (cs + CMP_LEN > js) & (np.arange(ncp)[None, :] < nc) & (np.arange(LANES)[:, None] < nsb)
    return jnp.asarray(ov.astype(np.float32), BF16)


def _expand_table_t(seq):
    e = (np.arange(seq)[:, None] // SEL_LEN) == np.arange(LANES)[None, :]
    return jnp.asarray(e.astype(np.float32), BF16)


def _layer0_weights(w_in):
    d = w_in.shape[0]
    qc = N_HEADS * HEAD_DIM
    kvc = 6 * N_KV * HEAD_DIM
    gc = 3 * N_HEADS
    gh = N_KV * HEAD_DIM
    wq = (w_in[:, :qc] * (HEAD_DIM ** -0.5 * np.log2(np.e))).reshape(d, N_KV, HPG, HEAD_DIM)
    zeros = jnp.zeros_like(wq[:, 0])
    wq_pad = jnp.stack([jnp.concatenate([wq[:, 0], zeros], axis=-1),
                        jnp.concatenate([zeros, wq[:, 1]], axis=-1)], axis=1)
    wqt = wq_pad.reshape(d, N_HEADS * LANES).T.astype(BF16)
    wkv = w_in[:, qc:qc + kvc]
    wvt = jnp.concatenate([wkv[:, 3 * gh:4 * gh], wkv[:, 5 * gh:6 * gh]], axis=1).T.astype(BF16)
    wg = w_in[:, qc + kvc:qc + kvc + gc].reshape(d, N_KV, 3 * HPG)
    wgt = jnp.pad(wg, ((0, 0), (0, 0), (0, LANES - 3 * HPG))).reshape(d, N_KV * LANES).T.astype(BF16)
    wglu = w_in[:, qc + kvc + gc:].astype(BF16)
    return wqt, wkv.astype(BF16), wvt, wgt, wglu


def _dispatch_plan(route, n_tok):
    tm = TM_MOE
    e_flat = route[:, 0:2].astype(jnp.int32).reshape(-1)
    onehot = (e_flat[:, None] == jnp.arange(N_EXPERTS, dtype=jnp.int32)[None, :]).astype(jnp.int32)
    rank = jnp.sum((jnp.cumsum(onehot, axis=0) - onehot) * onehot, axis=1)
    counts = jnp.sum(onehot, axis=0)
    padded = (counts + tm - 1) // tm * tm
    pad_ends = jnp.cumsum(padded)
    pad_starts = pad_ends - padded
    dest = (pad_starts[e_flat] + rank).astype(jnp.int32)
    n_rows = 2 * n_tok + (N_EXPERTS + 1) * tm
    n_tiles = n_rows // tm
    rowtok = jnp.zeros((n_rows,), jnp.int32).at[dest].set(jnp.arange(2 * n_tok, dtype=jnp.int32) // 2)
    nused = (pad_ends[-1] // tm).astype(jnp.int32)
    tiles = jnp.minimum(jnp.arange(n_tiles, dtype=jnp.int32), nused - 1) * tm
    tile_e = jnp.minimum(jnp.sum((pad_ends[None, :] <= tiles[:, None]).astype(jnp.int32), axis=1), N_EXPERTS - 1)
    return tile_e, nused.reshape(1), rowtok, dest


def kernel(x, norm_mix_pre, norm_mix_post, norm_ffn_pre, norm_ffn_post, nsa_conv_w_in, cmp_pe, cmp_w1, cmp_b1, cmp_w2, cmp_b2, conv_w, conv_b, conv_ln_g, conv_ln_b, nsa_conv_w_out, ffn_w_gate, ffn_w_up, ffn_w_down, gmlp_w_in, gmlp_ln_g, gmlp_ln_b, gmlp_w_s, gmlp_b_s, gmlp_w_out, moe_w_router, moe_w_gate, moe_w_up, moe_w_down):
    b, s, d = x.shape
    n = b * s
    assert d == D_MODEL and s // SEL_LEN == SEL_LEN and s % (2 * TK_SEL) == 0 and n % TM_PROJ == 0
    h0 = x.reshape(n, d)
    vec = lambda a: a.reshape(1, -1).astype(F32)

    wqt, wkv, wvt, wgt, wglu = _layer0_weights(nsa_conv_w_in[0])
    qt, kv, vt, gt, c = _proj0(h0, vec(norm_mix_pre[0]), wqt, wkv, wvt, wgt, wglu, b)
    kv3 = kv.reshape(b, s, 6 * N_KV * HEAD_DIM)

    nr = s // CMP_STRIDE
    kr = kv3[:, :, :2 * N_KV * HEAD_DIM].reshape(b, nr, CMP_STRIDE, 2, N_KV, HEAD_DIM)
    kr = kr.transpose(3, 0, 4, 1, 2, 5).reshape(2, b * N_KV, nr, CMP_STRIDE * HEAD_DIM)
    cmp = _compress(kr, cmp_pe[0].reshape(2, 1, CMP_LEN * HEAD_DIM).astype(F32), cmp_w1[0].astype(BF16),
                    cmp_b1[0].reshape(2, 1, CMP_HIDDEN).astype(F32), cmp_w2[0].astype(BF16),
                    cmp_b2[0].reshape(2, 1, HEAD_DIM).astype(F32))
    cmp = cmp.reshape(2, b, N_KV, nr, HEAD_DIM)
    kc = cmp[0].transpose(0, 2, 1, 3).reshape(b, nr, N_KV * HEAD_DIM)
    vct = cmp[1].transpose(0, 1, 3, 2).reshape(b, N_KV * HEAD_DIM, nr)

    o_attn = _attention(qt, kc, vct, kv3, vt, gt, _overlap_table_t(s), _expand_table_t(s))
    cc = _conv_branch(c.reshape(b, s, CONV_CH), conv_w[0].reshape(CONV_WIDTH, CONV_CH).astype(F32),
                      vec(conv_b[0]), vec(conv_ln_g[0]), vec(conv_ln_b[0]))
    w_out = nsa_conv_w_out[0].astype(BF16)
    nw = N_HEADS * HEAD_DIM
    h1, u1 = _outproj(o_attn.reshape(n, nw), cc.reshape(n, CONV_CH), w_out[:nw], w_out[nw:], h0,
                      vec(norm_mix_post[0]), vec(norm_ffn_pre[0]))
    h2, u2 = _ffn(u1, ffn_w_gate[0].astype(BF16), ffn_w_up[0].astype(BF16), ffn_w_down[0].astype(BF16), h1,
                  vec(norm_ffn_post[0]), vec(norm_mix_pre[1]))

    bsb = jnp.repeat(gmlp_b_s[0].T.astype(F32), d // GMLP_GROUPS, axis=1)
    wr = jnp.pad(moe_w_router[0].astype(F32), ((0, 0), (0, LANES - N_EXPERTS)))
    wr_hi = wr.astype(BF16)
    wr_lo = (wr - wr_hi.astype(F32)).astype(BF16)
    h3, u4, route = _gmlp(u2, gmlp_w_in[0].astype(BF16), vec(gmlp_ln_g[0]), vec(gmlp_ln_b[0]),
                          gmlp_w_s[0].astype(F32), bsb, gmlp_w_out[0].astype(BF16), h2,
                          vec(norm_mix_post[1]), vec(norm_ffn_pre[1]), wr_hi, wr_lo)
    tile_e, nused, rowtok, dest = _dispatch_plan(route, n)
    y = _experts(tile_e, nused, rowtok, u4, moe_w_gate[0].astype(BF16), moe_w_up[0].astype(BF16),
                 moe_w_down[0].astype(BF16))
    h4 = _combine(dest, y, route, h3, vec(norm_ffn_post[1]))
    return h4.reshape(b, s, d)
```

```python
import functools

import numpy as np
import jax
import jax.numpy as jnp
from jax import lax
from jax.experimental import pallas as pl
from jax.experimental.pallas import tpu as pltpu

F32 = jnp.float32
BF16 = jnp.bfloat16

D_MODEL = 1024
N_HEADS = 8
N_KV = 2
HPG = N_HEADS // N_KV
HEAD_DIM = 64
CMP_LEN = 32
CMP_STRIDE = 16
CMP_HIDDEN = 256
SEL_LEN = 64
N_SEL = 16
WINDOW = 512
CONV_CH = D_MODEL // 2
CONV_WIDTH = 31
GMLP_GROUPS = 8
GMLP_CHUNK = 128
N_EXPERTS = 8
EPS = 1e-6
NEG = -1e30

LANES = 128
TQ = 128
TK_SEL = 1024
TM_PROJ = 512
TS_CONV = 512
CONV_HALO = 32
TM_FFN = 512
TF_FFN = 1408
TM_MOE = 512
TF_MOE = 1792
TC_COMB = 256
VMEM_LIMIT = 56 * 1024 * 1024


def _params(sem):
    return pltpu.CompilerParams(dimension_semantics=sem, vmem_limit_bytes=VMEM_LIMIT)


def _dot(a, b):
    return jnp.dot(a, b, preferred_element_type=F32)


def _dot_nt(a, b):
    return lax.dot_general(a, b, (((1,), (1,)), ((), ())), preferred_element_type=F32)


def _rms(x, g):
    return x * lax.rsqrt(jnp.mean(x * x, axis=-1, keepdims=True) + EPS) * g


def _layer_norm(x, g, b):
    mu = jnp.mean(x, axis=-1, keepdims=True)
    xc = x - mu
    var = jnp.mean(xc * xc, axis=-1, keepdims=True)
    return xc * lax.rsqrt(var + EPS) * g + b


def _sigmoid(x):
    return 1.0 / (1.0 + jnp.exp(-x))


def _silu(x):
    return x * _sigmoid(x)


def _gelu_tanh(x):
    c = np.float32(np.sqrt(2.0 / np.pi))
    return 0.5 * x * (1.0 + jnp.tanh(c * (x + 0.044715 * (x * x * x))))


def _proj0_kernel(h_ref, g_ref, wqt_ref, wkv_ref, wvt_ref, wgt_ref, wglu_ref,
                  qt_ref, kv_ref, kcv_ref, vt_ref, gt_ref, c_ref):
    u = _rms(h_ref[...], g_ref[...]).astype(BF16)
    qt_ref[0] = _dot_nt(wqt_ref[...], u).astype(BF16)
    kv = _dot(u, wkv_ref[...])
    kv_ref[...] = kv.astype(BF16)
    kcv_ref[...] = kv[:, :kcv_ref.shape[1]]
    vt_ref[0] = _dot_nt(wvt_ref[...], u).astype(BF16)
    gt_ref[0] = _sigmoid(_dot_nt(wgt_ref[...], u))
    glu = _dot(u, wglu_ref[...])
    c_ref[...] = glu[:, :CONV_CH] * _sigmoid(glu[:, CONV_CH:])


def _proj0(h, g, wqt, wkv, wvt, wgt, wglu, batch):
    n, d = h.shape
    tm = TM_PROJ
    seq = n // batch
    per = seq // tm
    full = lambda a: pl.BlockSpec(a.shape, lambda i: (0,) * a.ndim)
    row = lambda c: pl.BlockSpec((tm, c), lambda i: (i, 0))
    col = lambda r: pl.BlockSpec((1, r, tm), lambda i: (i // per, 0, i % per))
    return pl.pallas_call(
        _proj0_kernel,
        grid=(n // tm,),
        in_specs=[row(d), full(g), full(wqt), full(wkv), full(wvt), full(wgt), full(wglu)],
        out_specs=[col(wqt.shape[0]), row(wkv.shape[1]), row(2 * LANES), col(wvt.shape[0]), col(wgt.shape[0]),
                   row(CONV_CH)],
        out_shape=[jax.ShapeDtypeStruct((batch, wqt.shape[0], seq), BF16),
                   jax.ShapeDtypeStruct((n, wkv.shape[1]), BF16),
                   jax.ShapeDtypeStruct((n, 2 * LANES), F32),
                   jax.ShapeDtypeStruct((batch, wvt.shape[0], seq), BF16),
                   jax.ShapeDtypeStruct((batch, wgt.shape[0], seq), F32),
                   jax.ShapeDtypeStruct((n, CONV_CH), F32)],
        compiler_params=_params(("parallel",)),
        name="proj0",
    )(h, g, wqt, wkv, wvt, wgt, wglu)


def _compress_kernel(x_ref, pe_ref, w1_ref, b1_ref, w2_ref, b2_ref, o_ref):
    nrow = o_ref.shape[2]
    top = jnp.zeros((nrow, w1_ref.shape[3]), F32)
    bot = jnp.zeros((nrow, w1_ref.shape[3]), F32)
    for l in range(CMP_STRIDE):
        x = x_ref[0, pl.ds(l, nrow, stride=CMP_STRIDE), :]
        top = top + _dot((x + pe_ref[0, l]).astype(BF16), w1_ref[0, l])
        bot = bot + _dot((x + pe_ref[0, CMP_STRIDE + l]).astype(BF16), w1_ref[0, CMP_STRIDE + l])
    pre = top + pltpu.roll(bot, nrow - 1, 0) + b1_ref[0]
    hid = _gelu_tanh(pre).astype(BF16)
    o_ref[0, 0] = (_dot(hid, w2_ref[0]) + b2_ref[0]).astype(o_ref.dtype)


def _compress(kcv, pe, w1, b1, w2, b2):
    b, s, _ = kcv.shape
    nr = s // CMP_STRIDE
    return pl.pallas_call(
        _compress_kernel,
        grid=(2, b),
        in_specs=[pl.BlockSpec((1, s, LANES), lambda j, i: (i, 0, j)),
                  pl.BlockSpec((1,) + pe.shape[1:], lambda j, i: (j, 0, 0, 0)),
                  pl.BlockSpec((1,) + w1.shape[1:], lambda j, i: (j, 0, 0, 0)),
                  pl.BlockSpec((1,) + b1.shape[1:], lambda j, i: (j, 0, 0)),
                  pl.BlockSpec((1,) + w2.shape[1:], lambda j, i: (j, 0, 0)),
                  pl.BlockSpec((1,) + b2.shape[1:], lambda j, i: (j, 0, 0))],
        out_specs=pl.BlockSpec((1, 1, nr, LANES), lambda j, i: (j, i, 0, 0)),
        out_shape=jax.ShapeDtypeStruct((2, b, nr, LANES), BF16),
        compiler_params=_params(("parallel", "parallel")),
        name="compress",
    )(kcv, pe, w1, b1, w2, b2)


def _compress_weights(cmp_pe, cmp_w1, cmp_b1, cmp_w2, cmp_b2):
    eye = jnp.eye(N_KV, dtype=F32)
    w1 = cmp_w1.reshape(2, CMP_LEN, HEAD_DIM, CMP_HIDDEN)
    w1 = jnp.einsum('gk,jldh->jlgdkh', eye, w1).reshape(2, CMP_LEN, N_KV * HEAD_DIM, N_KV * CMP_HIDDEN)
    w2 = jnp.einsum('gk,jhd->jghkd', eye, cmp_w2).reshape(2, N_KV * CMP_HIDDEN, N_KV * HEAD_DIM)
    pe = jnp.tile(cmp_pe.reshape(2, CMP_LEN, 1, HEAD_DIM), (1, 1, 1, N_KV)).astype(F32)
    b1 = jnp.tile(cmp_b1.reshape(2, 1, CMP_HIDDEN), (1, 1, N_KV)).astype(F32)
    b2 = jnp.tile(cmp_b2.reshape(2, 1, HEAD_DIM), (1, 1, N_KV)).astype(F32)
    return pe, w1.astype(BF16), b1, w2.astype(BF16), b2


def _col_reduce(x, op):
    rows = x.shape[0]
    part = op(x.reshape(4, rows // 4, x.shape[1]), axis=0)
    return op(part, axis=0, keepdims=True)


def _attn_kernel(qt_ref, kc_ref, vct_ref, ks_ref, vst_ref, kw_ref, vwt_ref, gt_ref,
                 ovt_ref, et_ref, o_ref, rhs_ref, m_ref, l_ref, acc_ref):
    qi = pl.program_id(1)
    q0 = qi * TQ
    nsb = SEL_LEN
    ncp = kc_ref.shape[1]
    wkeys = WINDOW + TQ
    groups = range(N_KV)
    tile4 = lambda a: jnp.concatenate([a] * HPG, axis=1)
    qgs = [jnp.concatenate(
        [qt_ref[0, LANES * (HPG * g + h):LANES * (HPG * g + h + 1), :] for h in range(HPG)], axis=1)
        for g in groups]

    n_idx = lax.broadcasted_iota(jnp.int32, (ncp, TQ), 0)
    t_idx = q0 + lax.broadcasted_iota(jnp.int32, (ncp, TQ), 1)
    cmask = tile4(n_idx * CMP_STRIDE + (CMP_LEN - 1) <= t_idx)
    o_c, imp_t = [], []
    for g in groups:
        s = jnp.where(cmask, _dot(kc_ref[0], qgs[g]), NEG)
        m = _col_reduce(s, jnp.max)
        p = jnp.where(cmask, jnp.exp2(s - m), 0.0)
        l = _col_reduce(p, jnp.sum)
        p = p / jnp.where(l > 0.0, l, 1.0)
        o_c.append(_dot(vct_ref[0], p.astype(BF16)))
        psum = p[:, 0:TQ] + p[:, TQ:2 * TQ] + p[:, 2 * TQ:3 * TQ] + p[:, 3 * TQ:4 * TQ]
        imp_t.append(_dot(ovt_ref[...], psum.astype(BF16))[:nsb])

    j_idx = lax.broadcasted_iota(jnp.int32, (nsb, TQ), 0)
    cur = (q0 + lax.broadcasted_iota(jnp.int32, (nsb, TQ), 1)) // SEL_LEN
    valid = j_idx <= cur
    forced = (j_idx == 0) | (j_idx == cur) | (j_idx == cur - 1)
    sub = lax.broadcasted_iota(jnp.int32, (8, TQ), 0)
    nch = nsb // 8
    for g in groups:
        score = jnp.where(valid, imp_t[g] + jnp.where(forced, 1e6, 0.0), -1e9)
        chunks = [score[8 * v:8 * v + 8] for v in range(nch)]
        counts = [jnp.zeros((8, TQ), F32) for _ in range(nch)]
        for i in range(nsb):
            row = jnp.broadcast_to(score[i:i + 1], (8, TQ))
            for v in range(nch):
                if i < 8 * v:
                    ahead = row >= chunks[v]
                elif i >= 8 * v + 8:
                    ahead = row > chunks[v]
                else:
                    ahead = jnp.where(sub > (i - 8 * v),
                                      jnp.where(row >= chunks[v], 1.0, 0.0),
                                      jnp.where(row > chunks[v], 1.0, 0.0)) > 0.5
                counts[v] = counts[v] + jnp.where(ahead, 1.0, 0.0)
        bias_t = jnp.concatenate(
            [jnp.where((c < float(N_SEL)) & valid[8 * v:8 * v + 8], 0.0, NEG) for v, c in enumerate(counts)]
            + [jnp.zeros((LANES - nsb, TQ), F32)], axis=0).astype(BF16)
        rhs_ref[g] = jnp.concatenate([qgs[g], tile4(bias_t)], axis=0)

    m_ref[...] = jnp.full(m_ref.shape, NEG, F32)
    l_ref[...] = jnp.zeros(l_ref.shape, F32)
    acc_ref[...] = jnp.zeros(acc_ref.shape, F32)

    def sel_tile(kt, diagonal):
        k0 = pl.multiple_of(kt * TK_SEL, TK_SEL)
        lhs = jnp.concatenate([ks_ref[0, pl.ds(k0, TK_SEL), :], et_ref[pl.ds(k0, TK_SEL), :]], axis=1)
        vt = vst_ref[0, :, pl.ds(k0, TK_SEL)]
        if diagonal:
            kpos = k0 + lax.broadcasted_iota(jnp.int32, (TK_SEL, TQ), 0)
            tpos = q0 + lax.broadcasted_iota(jnp.int32, (TK_SEL, TQ), 1)
            causal = tile4(kpos <= tpos)
        for g in groups:
            sc = _dot(lhs, rhs_ref[g])
            if diagonal:
                sc = jnp.where(causal, sc, NEG)
            m_i = m_ref[g]
            m_new = jnp.maximum(m_i, _col_reduce(sc, jnp.max))
            alpha = jnp.exp2(m_i - m_new)
            pexp = jnp.exp2(sc - m_new)
            l_ref[g] = alpha * l_ref[g] + _col_reduce(pexp, jnp.sum)
            acc_ref[g] = alpha * acc_ref[g] + _dot(vt, pexp.astype(BF16))
            m_ref[g] = m_new

    last = q0 // TK_SEL

    def sel_body(kt, carry):
        sel_tile(kt, False)
        return carry

    lax.fori_loop(0, last, sel_body, 0)
    sel_tile(last, True)

    k0 = pl.multiple_of(jnp.maximum(q0 - WINDOW, 0), TQ)
    kpos = k0 + lax.broadcasted_iota(jnp.int32, (wkeys, TQ), 0)
    tpos = q0 + lax.broadcasted_iota(jnp.int32, (wkeys, TQ), 1)
    wmask = tile4((kpos <= tpos) & (kpos > tpos - WINDOW))
    kwin = kw_ref[0, pl.ds(k0, wkeys), :]
    vwin = vwt_ref[0, :, pl.ds(k0, wkeys)]
    o_w = []
    for g in groups:
        sc = jnp.where(wmask, _dot(kwin, qgs[g]), NEG)
        pexp = jnp.exp2(sc - _col_reduce(sc, jnp.max))
        o_w.append(_dot(vwin, pexp.astype(BF16)) / _col_reduce(pexp, jnp.sum))

    blocks = []
    for g in groups:
        o_s = acc_ref[g] / l_ref[g]
        for h in range(HPG):
            r0 = LANES * g + 3 * h
            cols = slice(h * TQ, (h + 1) * TQ)
            og = (gt_ref[0, r0:r0 + 1, :] * o_c[g][:, cols] + gt_ref[0, r0 + 1:r0 + 2, :] * o_s[:, cols]
                  + gt_ref[0, r0 + 2:r0 + 3, :] * o_w[g][:, cols])
            blocks.append(og[HEAD_DIM * g:HEAD_DIM * (g + 1)])
    o_ref[0] = jnp.concatenate(blocks, axis=0).T.astype(o_ref.dtype)


def _attention(qt, kc, vct, kv, vt, gt, ovt, et):
    b, _, s = qt.shape
    ncp = kc.shape[1]
    kspec = lambda j: pl.BlockSpec((1, s, LANES), lambda bi, qi: (bi, 0, j))
    vspec = lambda j: pl.BlockSpec((1, LANES, s), lambda bi, qi: (bi, j, 0))
    return pl.pallas_call(
        _attn_kernel,
        grid=(b, s // TQ),
        in_specs=[pl.BlockSpec((1, N_HEADS * LANES, TQ), lambda bi, qi: (bi, 0, qi)),
                  pl.BlockSpec((1, ncp, LANES), lambda bi, qi: (bi, 0, 0)),
                  pl.BlockSpec((1, LANES, ncp), lambda bi, qi: (bi, 0, 0)),
                  kspec(2), vspec(0), kspec(4), vspec(1),
                  pl.BlockSpec((1, N_KV * LANES, TQ), lambda bi, qi: (bi, 0, qi)),
                  pl.BlockSpec(ovt.shape, lambda bi, qi: (0, 0)),
                  pl.BlockSpec(et.shape, lambda bi, qi: (0, 0))],
        out_specs=pl.BlockSpec((1, TQ, N_HEADS * HEAD_DIM), lambda bi, qi: (bi, qi, 0)),
        out_shape=jax.ShapeDtypeStruct((b, s, N_HEADS * HEAD_DIM), BF16),
        scratch_shapes=[pltpu.VMEM((N_KV, 2 * LANES, HPG * TQ), BF16),
                        pltpu.VMEM((N_KV, 1, HPG * TQ), F32), pltpu.VMEM((N_KV, 1, HPG * TQ), F32),
                        pltpu.VMEM((N_KV, LANES, HPG * TQ), F32)],
        compiler_params=_params(("parallel", "arbitrary")),
        name="nsa_attention",
    )(qt, kc, vct, kv, vt, kv, vt, gt, ovt, et)


def _conv_kernel(c_ref, halo_ref, cw_ref, cb_ref, g_ref, b_ref, o_ref, xs_ref):
    i = pl.program_id(1)
    ts = c_ref.shape[1]
    xs_ref[0:CONV_HALO, :] = jnp.where(i > 0, halo_ref[0], 0.0)
    xs_ref[CONV_HALO:CONV_HALO + ts, :] = c_ref[0]
    acc = jnp.zeros((ts, CONV_CH), F32)
    for w in range(CONV_WIDTH):
        off = CONV_HALO - (CONV_WIDTH - 1) + w
        acc = acc + xs_ref[off:off + ts, :] * cw_ref[w:w + 1, :]
    y = _layer_norm(acc + cb_ref[...], g_ref[...], b_ref[...])
    o_ref[0] = _silu(y).astype(o_ref.dtype)


def _conv_branch(c, cw, cb, ln_g, ln_b):
    b, s, ch = c.shape
    ts = TS_CONV
    per = ts // CONV_HALO
    vec = pl.BlockSpec((1, ch), lambda bi, i: (0, 0))
    return pl.pallas_call(
        _conv_kernel,
        grid=(b, s // ts),
        in_specs=[pl.BlockSpec((1, ts, ch), lambda bi, i: (bi, i, 0)),
                  pl.BlockSpec((1, CONV_HALO, ch), lambda bi, i: (bi, jnp.maximum(i * per - 1, 0), 0)),
                  pl.BlockSpec((CONV_WIDTH, ch), lambda bi, i: (0, 0)), vec, vec, vec],
        out_specs=pl.BlockSpec((1, ts, ch), lambda bi, i: (bi, i, 0)),
        out_shape=jax.ShapeDtypeStruct((b, s, ch), BF16),
        scratch_shapes=[pltpu.VMEM((CONV_HALO + ts, ch), F32)],
        compiler_params=_params(("parallel", "arbitrary")),
        name="conv_branch",
    )(c, c, cw, cb, ln_g, ln_b)


def _outproj_kernel(oa_ref, cc_ref, wt_ref, wb_ref, h_ref, gpost_ref, gnext_ref, h1_ref, u_ref):
    m = _dot(oa_ref[...], wt_ref[...]) + _dot(cc_ref[...], wb_ref[...])
    h1 = h_ref[...] + _rms(m, gpost_ref[...])
    h1_ref[...] = h1
    u_ref[...] = _rms(h1, gnext_ref[...]).astype(u_ref.dtype)


def _outproj(oa, cc, wt, wb, h, gpost, gnext):
    n, d = h.shape
    tm = TM_PROJ
    full = lambda a: pl.BlockSpec(a.shape, lambda i: (0,) * a.ndim)
    row = lambda c: pl.BlockSpec((tm, c), lambda i: (i, 0))
    return pl.pallas_call(
        _outproj_kernel,
        grid=(n // tm,),
        in_specs=[row(oa.shape[1]), row(cc.shape[1]), full(wt), full(wb), row(d), full(gpost), full(gnext)],
        out_specs=[row(d), row(d)],
        out_shape=[jax.ShapeDtypeStruct((n, d), F32), jax.ShapeDtypeStruct((n, d), BF16)],
        compiler_params=_params(("parallel",)),
        name="outproj",
    )(oa, cc, wt, wb, h, gpost, gnext)


def _ffn_kernel(u_ref, wg_ref, wu_ref, wd_ref, h_ref, gpost_ref, gnext_ref, h2_ref, u2_ref, acc_ref):
    f = pl.program_id(1)

    @pl.when(f == 0)
    def _():
        acc_ref[...] = jnp.zeros_like(acc_ref)

    u = u_ref[...]
    act = (_silu(_dot(u, wg_ref[...])) * _dot(u, wu_ref[...])).astype(BF16)
    acc_ref[...] += _dot(act, wd_ref[...])

    @pl.when(f == pl.num_programs(1) - 1)
    def _():
        h2 = h_ref[...] + _rms(acc_ref[...], gpost_ref[...])
        h2_ref[...] = h2
        u2_ref[...] = _rms(h2, gnext_ref[...]).astype(u2_ref.dtype)


def _ffn(u, wg, wu, wd, h, gpost, gnext):
    n, d = h.shape
    ff = wg.shape[1]
    tm, tf = TM_FFN, TF_FFN
    row = pl.BlockSpec((tm, d), lambda i, f: (i, 0))
    vec = pl.BlockSpec((1, d), lambda i, f: (0, 0))
    return pl.pallas_call(
        _ffn_kernel,
        grid=(n // tm, ff // tf),
        in_specs=[row, pl.BlockSpec((d, tf), lambda i, f: (0, f)), pl.BlockSpec((d, tf), lambda i, f: (0, f)),
                  pl.BlockSpec((tf, d), lambda i, f: (f, 0)), row, vec, vec],
        out_specs=[row, row],
        out_shape=[jax.ShapeDtypeStruct((n, d), F32), jax.ShapeDtypeStruct((n, d), BF16)],
        scratch_shapes=[pltpu.VMEM((tm, d), F32)],
        compiler_params=_params(("parallel", "arbitrary")),
        name="dense_ffn",
    )(u, wg, wu, wd, h, gpost, gnext)


def _gmlp_kernel(u_ref, win_ref, lng_ref, lnb_ref, ws_ref, bs_ref, wout_ref, h_ref, gpost_ref, gnext_ref,
                 wrh_ref, wrl_ref, h3_ref, u4_ref, route_ref, mix_ref):
    tm = u_ref.shape[0]
    width = wout_ref.shape[0]
    z = _gelu_tanh(_dot(u_ref[...], win_ref[...]))
    z1 = z[:, :width]
    z2 = _layer_norm(z[:, width:], lng_ref[...], lnb_ref[...]).astype(BF16)
    r_idx = lax.broadcasted_iota(jnp.int32, (GMLP_CHUNK, GMLP_CHUNK), 0)
    c_idx = lax.broadcasted_iota(jnp.int32, (GMLP_CHUNK, GMLP_CHUNK), 1)
    gw = width // GMLP_GROUPS
    for g in range(GMLP_GROUPS):
        wsg = jnp.where(c_idx <= r_idx, ws_ref[g], 0.0).astype(BF16)
        for c in range(tm // GMLP_CHUNK):
            rows = slice(c * GMLP_CHUNK, (c + 1) * GMLP_CHUNK)
            cols = slice(g * gw, (g + 1) * gw)
            mix_ref[rows, cols] = _dot(wsg, z2[rows, cols]) + bs_ref[:, cols]
    gated = (z1 * mix_ref[...]).astype(BF16)
    y = _dot(gated, wout_ref[...])
    h3 = h_ref[...] + _rms(y, gpost_ref[...])
    h3_ref[...] = h3
    u4 = _rms(h3, gnext_ref[...])
    u4_ref[...] = u4

    u_hi = u4.astype(BF16)
    u_lo = (u4 - u_hi.astype(F32)).astype(BF16)
    logits = _dot(u_hi, wrh_ref[...]) + (_dot(u_lo, wrh_ref[...]) + _dot(u_hi, wrl_ref[...]))
    lane = lax.broadcasted_iota(jnp.int32, (tm, LANES), 1).astype(F32)
    lg = jnp.where(lane < float(N_EXPERTS), logits, NEG)
    m1 = jnp.max(lg, axis=-1, keepdims=True)
    i1 = jnp.min(jnp.where(lg == m1, lane, float(LANES)), axis=-1, keepdims=True)
    lg2 = jnp.where(lane == i1, NEG, lg)
    m2 = jnp.max(lg2, axis=-1, keepdims=True)
    i2 = jnp.min(jnp.where(lg2 == m2, lane, float(LANES)), axis=-1, keepdims=True)
    e2 = jnp.exp(m2 - m1)
    den = 1.0 + e2
    route_ref[...] = jnp.where(lane == 0.0, i1, jnp.where(lane == 1.0, i2,
                               jnp.where(lane == 2.0, 1.0 / den, jnp.where(lane == 3.0, e2 / den, 0.0))))


def _gmlp(u, win, lng, lnb, ws, bsb, wout, h, gpost, gnext, wrh, wrl):
    n, d = h.shape
    tm = TM_PROJ
    full = lambda a: pl.BlockSpec(a.shape, lambda i: (0,) * a.ndim)
    row = lambda c: pl.BlockSpec((tm, c), lambda i: (i, 0))
    return pl.pallas_call(
        _gmlp_kernel,
        grid=(n // tm,),
        in_specs=[row(d), full(win), full(lng), full(lnb), full(ws), full(bsb), full(wout), row(d),
                  full(gpost), full(gnext), full(wrh), full(wrl)],
        out_specs=[row(d), row(d), row(LANES)],
        out_shape=[jax.ShapeDtypeStruct((n, d), F32), jax.ShapeDtypeStruct((n, d), F32),
                   jax.ShapeDtypeStruct((n, LANES), F32)],
        scratch_shapes=[pltpu.VMEM((tm, wout.shape[0]), F32)],
        compiler_params=_params(("parallel",)),
        name="gmlp_router",
    )(u, win, lng, lnb, ws, bsb, wout, h, gpost, gnext, wrh, wrl)


def _expert_kernel(tile_e_ref, nused_ref, rowtok_ref, x_hbm, wg_ref, wu_ref, wd_ref, y_ref,
                   xg_ref, xb_ref, acc_ref, sem, *, nf):
    i = pl.program_id(0)
    f = pl.program_id(1)
    tm = xg_ref.shape[1]
    nused = nused_ref[0]
    active = i < nused
    slot = i % 2

    def row_copy(tok, s, r):
        return pltpu.make_async_copy(x_hbm.at[pl.ds(tok, 1)], xg_ref.at[s, pl.ds(r, 1)], sem.at[s])

    def wait_tile(s):
        pltpu.make_async_copy(x_hbm.at[pl.ds(0, tm)], xg_ref.at[s], sem.at[s]).wait()

    @pl.when((i == 0) & (f == 0))
    def _():
        def body(r, carry):
            row_copy(rowtok_ref[r], 0, r).start()
            return carry

        lax.fori_loop(0, tm, body, 0, unroll=8)

    @pl.when((i <= nused) & (f == 0))
    def _():
        wait_tile(slot)

    @pl.when(active & (f == 0))
    def _():
        xb_ref[...] = xg_ref[slot].astype(BF16)
        acc_ref[...] = jnp.zeros_like(acc_ref)

    @pl.when(active)
    def _():
        per = tm // nf
        base = (i + 1) * tm + f * per
        for j in range(per):
            row_copy(rowtok_ref[base + j], 1 - slot, f * per + j).start()
        x = xb_ref[...]
        act = (_silu(_dot(x, wg_ref[0])) * _dot(x, wu_ref[0])).astype(BF16)
        acc_ref[...] += _dot(act, wd_ref[0])

    last = f == pl.num_programs(1) - 1

    @pl.when(active & last)
    def _():
        y_ref[...] = acc_ref[...]

    @pl.when(jnp.logical_not(active) & last)
    def _():
        y_ref[...] = jnp.zeros_like(y_ref)


def _experts(tile_e, nused, rowtok, x, wg, wu, wd):
    n_rows = rowtok.shape[0]
    d = x.shape[1]
    ff = wg.shape[2]
    tm, tf = TM_MOE, TF_MOE
    nf = ff // tf

    def fidx(i, f, te, nu):
        return jnp.where(i < nu[0], f, nf - 1)

    grid_spec = pltpu.PrefetchScalarGridSpec(
        num_scalar_prefetch=3,
        grid=(n_rows // tm, nf),
        in_specs=[pl.BlockSpec(memory_space=pl.ANY),
                  pl.BlockSpec((1, d, tf), lambda i, f, te, nu, rt: (te[i], 0, fidx(i, f, te, nu))),
                  pl.BlockSpec((1, d, tf), lambda i, f, te, nu, rt: (te[i], 0, fidx(i, f, te, nu))),
                  pl.BlockSpec((1, tf, d), lambda i, f, te, nu, rt: (te[i], fidx(i, f, te, nu), 0))],
        out_specs=pl.BlockSpec((tm, d), lambda i, f, te, nu, rt: (i, 0)),
        scratch_shapes=[pltpu.VMEM((2, tm, d), F32), pltpu.VMEM((tm, d), BF16), pltpu.VMEM((tm, d), F32),
                        pltpu.SemaphoreType.DMA((2,))],
    )
    return pl.pallas_call(
        functools.partial(_expert_kernel, nf=nf),
        grid_spec=grid_spec,
        out_shape=jax.ShapeDtypeStruct((n_rows, d), F32),
        compiler_params=_params(("arbitrary", "arbitrary")),
        name="moe_experts",
    )(tile_e, nused, rowtok, x, wg, wu, wd)


def _combine_kernel(dest_ref, y_hbm, route_ref, h_ref, gpost_ref, o_ref, ya_ref, yb_ref, sem):
    i = pl.program_id(0)
    tc = ya_ref.shape[1]
    slot = i % 2

    def copies(d0, d1, s, r):
        return (pltpu.make_async_copy(y_hbm.at[pl.ds(d0, 1)], ya_ref.at[s, pl.ds(r, 1)], sem.at[0, s]),
                pltpu.make_async_copy(y_hbm.at[pl.ds(d1, 1)], yb_ref.at[s, pl.ds(r, 1)], sem.at[1, s]))

    def issue_tile(t, s):
        for r in range(tc):
            p = 2 * (t * tc + r)
            ca, cb = copies(dest_ref[p], dest_ref[p + 1], s, r)
            ca.start()
            cb.start()

    def wait_tile(s):
        pltpu.make_async_copy(y_hbm.at[pl.ds(0, tc)], ya_ref.at[s], sem.at[0, s]).wait()
        pltpu.make_async_copy(y_hbm.at[pl.ds(0, tc)], yb_ref.at[s], sem.at[1, s]).wait()

    @pl.when(i == 0)
    def _():
        issue_tile(0, 0)

    @pl.when(i + 1 < pl.num_programs(0))
    def _():
        issue_tile(i + 1, 1 - slot)

    wait_tile(slot)
    route = route_ref[...]
    moe = route[:, 2:3] * ya_ref[slot] + route[:, 3:4] * yb_ref[slot]
    o_ref[...] = h_ref[...] + _rms(moe, gpost_ref[...])


def _combine(dest, y, route, h, gpost):
    n, d = h.shape
    tc = TC_COMB
    grid_spec = pltpu.PrefetchScalarGridSpec(
        num_scalar_prefetch=1,
        grid=(n // tc,),
        in_specs=[pl.BlockSpec(memory_space=pl.ANY),
                  pl.BlockSpec((tc, LANES), lambda i, ds: (i, 0)),
                  pl.BlockSpec((tc, d), lambda i, ds: (i, 0)),
                  pl.BlockSpec((1, d), lambda i, ds: (0, 0))],
        out_specs=pl.BlockSpec((tc, d), lambda i, ds: (i, 0)),
        scratch_shapes=[pltpu.VMEM((2, tc, d), F32), pltpu.VMEM((2, tc, d), F32),
                        pltpu.SemaphoreType.DMA((2, 2))],
    )
    return pl.pallas_call(
        _combine_kernel,
        grid_spec=grid_spec,
        out_shape=jax.ShapeDtypeStruct((n, d), F32),
        compiler_params=_params(("arbitrary",)),
        name="moe_combine",
    )(dest, y, route, h, gpost)


def _overlap_table_t(seq):
    nc = (seq - CMP_LEN) // CMP_STRIDE + 1
    ncp = seq // CMP_STRIDE
    nsb = seq // SEL_LEN
    cs = np.arange(ncp)[None, :] * CMP_STRIDE
    js = np.arange(LANES)[:, None] * SEL_LEN
    ov = (cs < js + SEL_LEN) & (cs + CMP_LEN > js) & (np.arange(ncp)[None, :] < nc) & (np.arange(LANES)[:, None] < nsb)
    return jnp.asarray(ov.astype(np.float32), BF16)


def _expand_table_t(seq):
    e = (np.arange(seq)[:, None] // SEL_LEN) == np.arange(LANES)[None, :]
    return jnp.asarray(e.astype(np.float32), BF16)


def _layer0_weights(w_in):
    d = w_in.shape[0]
    qc = N_HEADS * HEAD_DIM
    kvc = 6 * N_KV * HEAD_DIM
    gc = 3 * N_HEADS
    gh = N_KV * HEAD_DIM
    wq = (w_in[:, :qc] * (HEAD_DIM ** -0.5 * np.log2(np.e))).reshape(d, N_KV, HPG, HEAD_DIM)
    zeros = jnp.zeros_like(wq[:, 0])
    wq_pad = jnp.stack([jnp.concatenate([wq[:, 0], zeros], axis=-1),
                        jnp.concatenate([zeros, wq[:, 1]], axis=-1)], axis=1)
    wqt = wq_pad.reshape(d, N_HEADS * LANES).T.astype(BF16)
    wkv = w_in[:, qc:qc + kvc]
    wvt = jnp.concatenate([wkv[:, 3 * gh:4 * gh], wkv[:, 5 * gh:6 * gh]], axis=1).T.astype(BF16)
    wg = w_in[:, qc + kvc:qc + kvc + gc].reshape(d, N_KV, 3 * HPG)
    wgt = jnp.pad(wg, ((0, 0), (0, 0), (0, LANES - 3 * HPG))).reshape(d, N_KV * LANES).T.astype(BF16)
    wglu = w_in[:, qc + kvc + gc:].astype(BF16)
    return wqt, wkv.astype(BF16), wvt, wgt, wglu


def _dispatch_plan(route, n_tok):
    tm = TM_MOE
    e_flat = route[:, 0:2].astype(jnp.int32).reshape(-1)
    onehot = (e_flat[:, None] == jnp.arange(N_EXPERTS, dtype=jnp.int32)[None, :]).astype(jnp.int32)
    rank = jnp.sum((jnp.cumsum(onehot, axis=0) - onehot) * onehot, axis=1)
    counts = jnp.sum(onehot, axis=0)
    padded = (counts + tm - 1) // tm * tm
    pad_ends = jnp.cumsum(padded)
    pad_starts = pad_ends - padded
    dest = (pad_starts[e_flat] + rank).astype(jnp.int32)
    n_rows = 2 * n_tok + (N_EXPERTS + 1) * tm
    n_tiles = n_rows // tm
    rowtok = jnp.zeros((n_rows,), jnp.int32).at[dest].set(jnp.arange(2 * n_tok, dtype=jnp.int32) // 2)
    nused = (pad_ends[-1] // tm).astype(jnp.int32)
    tiles = jnp.minimum(jnp.arange(n_tiles, dtype=jnp.int32), nused - 1) * tm
    tile_e = jnp.minimum(jnp.sum((pad_ends[None, :] <= tiles[:, None]).astype(jnp.int32), axis=1), N_EXPERTS - 1)
    return tile_e, nused.reshape(1), rowtok, dest


def kernel(x, norm_mix_pre, norm_mix_post, norm_ffn_pre, norm_ffn_post, nsa_conv_w_in, cmp_pe, cmp_w1, cmp_b1, cmp_w2, cmp_b2, conv_w, conv_b, conv_ln_g, conv_ln_b, nsa_conv_w_out, ffn_w_gate, ffn_w_up, ffn_w_down, gmlp_w_in, gmlp_ln_g, gmlp_ln_b, gmlp_w_s, gmlp_b_s, gmlp_w_out, moe_w_router, moe_w_gate, moe_w_up, moe_w_down):
    b, s, d = x.shape
    n = b * s
    assert d == D_MODEL and s // SEL_LEN == SEL_LEN and s % (2 * TK_SEL) == 0 and n % TM_PROJ == 0
    h0 = x.reshape(n, d)
    vec = lambda a: a.reshape(1, -1).astype(F32)

    wqt, wkv, wvt, wgt, wglu = _layer0_weights(nsa_conv_w_in[0])
    qt, kv, kcv, vt, gt, c = _proj0(h0, vec(norm_mix_pre[0]), wqt, wkv, wvt, wgt, wglu, b)
    kv3 = kv.reshape(b, s, 6 * N_KV * HEAD_DIM)
    cmp = _compress(kcv.reshape(b, s, 2 * LANES), *_compress_weights(cmp_pe[0], cmp_w1[0], cmp_b1[0],
                                                                     cmp_w2[0], cmp_b2[0]))
    kc = cmp[0]
    vct = cmp[1].transpose(0, 2, 1)

    o_attn = _attention(qt, kc, vct, kv3, vt, gt, _overlap_table_t(s), _expand_table_t(s))
    cc = _conv_branch(c.reshape(b, s, CONV_CH), conv_w[0].reshape(CONV_WIDTH, CONV_CH).astype(F32),
                      vec(conv_b[0]), vec(conv_ln_g[0]), vec(conv_ln_b[0]))
    w_out = nsa_conv_w_out[0].astype(BF16)
    nw = N_HEADS * HEAD_DIM
    h1, u1 = _outproj(o_attn.reshape(n, nw), cc.reshape(n, CONV_CH), w_out[:nw], w_out[nw:], h0,
                      vec(norm_mix_post[0]), vec(norm_ffn_pre[0]))
    h2, u2 = _ffn(u1, ffn_w_gate[0].astype(BF16), ffn_w_up[0].astype(BF16), ffn_w_down[0].astype(BF16), h1,
                  vec(norm_ffn_post[0]), vec(norm_mix_pre[1]))

    bsb = jnp.repeat(gmlp_b_s[0].T.astype(F32), d // GMLP_GROUPS, axis=1)
    wr = jnp.pad(moe_w_router[0].astype(F32), ((0, 0), (0, LANES - N_EXPERTS)))
    wr_hi = wr.astype(BF16)
    wr_lo = (wr - wr_hi.astype(F32)).astype(BF16)
    h3, u4, route = _gmlp(u2, gmlp_w_in[0].astype(BF16), vec(gmlp_ln_g[0]), vec(gmlp_ln_b[0]),
                          gmlp_w_s[0].astype(F32), bsb, gmlp_w_out[0].astype(BF16), h2,
                          vec(norm_mix_post[1]), vec(norm_ffn_pre[1]), wr_hi, wr_lo)
    tile_e, nused, rowtok, dest = _dispatch_plan(route, n)
    y = _experts(tile_e, nused, rowtok, u4, moe_w_gate[0].astype(BF16), moe_w_up[0].astype(BF16),
                 moe_w_down[0].astype(BF16))
    h4 = _combine(dest, y, route, h3, vec(norm_ffn_post[1]))
    return h4.reshape(b, s, d)
```

```python
import functools

import numpy as np
import jax
import jax.numpy as jnp
from jax import lax
from jax.experimental import pallas as pl
from jax.experimental.pallas import tpu as pltpu

F32 = jnp.float32
BF16 = jnp.bfloat16

D_MODEL = 1024
N_HEADS = 8
N_KV = 2
HPG = N_HEADS // N_KV
HEAD_DIM = 64
CMP_LEN = 32
CMP_STRIDE = 16
CMP_HIDDEN = 256
SEL_LEN = 64
N_SEL = 16
WINDOW = 512
CONV_CH = D_MODEL // 2
CONV_WIDTH = 31
GMLP_GROUPS = 8
GMLP_CHUNK = 128
N_EXPERTS = 8
EPS = 1e-6
NEG = -1e30

LANES = 128
TQ = 128
TK_SEL = 1024
TM_PROJ = 512
TS_CONV = 512
CONV_HALO = 32
TM_FFN = 256
TF_FFN = 2816
TM_MOE = 512
TF_MOE = 1792
TC_COMB = 256
VMEM_LIMIT = 56 * 1024 * 1024


def _params(sem):
    return pltpu.CompilerParams(dimension_semantics=sem, vmem_limit_bytes=VMEM_LIMIT)


def _dot(a, b):
    return jnp.dot(a, b, preferred_element_type=F32)


def _dot_nt(a, b):
    return lax.dot_general(a, b, (((1,), (1,)), ((), ())), preferred_element_type=F32)


def _rms(x, g):
    return x * lax.rsqrt(jnp.mean(x * x, axis=-1, keepdims=True) + EPS) * g


def _layer_norm(x, g, b):
    mu = jnp.mean(x, axis=-1, keepdims=True)
    xc = x - mu
    var = jnp.mean(xc * xc, axis=-1, keepdims=True)
    return xc * lax.rsqrt(var + EPS) * g + b


def _sigmoid(x):
    return 1.0 / (1.0 + jnp.exp(-x))


def _silu(x):
    return x * _sigmoid(x)


def _gelu_tanh(x):
    c = np.float32(np.sqrt(2.0 / np.pi))
    return 0.5 * x * (1.0 + jnp.tanh(c * (x + 0.044715 * (x * x * x))))


def _proj0_kernel(h_ref, g_ref, wqt_ref, wkv_ref, wvt_ref, wgt_ref, wglu_ref,
                  qt_ref, kv_ref, kcv_ref, vt_ref, gt_ref, c_ref):
    u = _rms(h_ref[...], g_ref[...]).astype(BF16)
    qt_ref[0] = _dot_nt(wqt_ref[...], u).astype(BF16)
    kv = _dot(u, wkv_ref[...])
    kv_ref[...] = kv.astype(BF16)
    kcv_ref[...] = kv[:, :kcv_ref.shape[1]]
    vt_ref[0] = _dot_nt(wvt_ref[...], u).astype(BF16)
    gt_ref[0] = _sigmoid(_dot_nt(wgt_ref[...], u))
    glu = _dot(u, wglu_ref[...])
    c_ref[...] = glu[:, :CONV_CH] * _sigmoid(glu[:, CONV_CH:])


def _proj0(h, g, wqt, wkv, wvt, wgt, wglu, batch):
    n, d = h.shape
    tm = TM_PROJ
    seq = n // batch
    per = seq // tm
    full = lambda a: pl.BlockSpec(a.shape, lambda i: (0,) * a.ndim)
    row = lambda c: pl.BlockSpec((tm, c), lambda i: (i, 0))
    col = lambda r: pl.BlockSpec((1, r, tm), lambda i: (i // per, 0, i % per))
    return pl.pallas_call(
        _proj0_kernel,
        grid=(n // tm,),
        in_specs=[row(d), full(g), full(wqt), full(wkv), full(wvt), full(wgt), full(wglu)],
        out_specs=[col(wqt.shape[0]), row(wkv.shape[1]), row(2 * LANES), col(wvt.shape[0]), col(wgt.shape[0]),
                   row(CONV_CH)],
        out_shape=[jax.ShapeDtypeStruct((batch, wqt.shape[0], seq), BF16),
                   jax.ShapeDtypeStruct((n, wkv.shape[1]), BF16),
                   jax.ShapeDtypeStruct((n, 2 * LANES), F32),
                   jax.ShapeDtypeStruct((batch, wvt.shape[0], seq), BF16),
                   jax.ShapeDtypeStruct((batch, wgt.shape[0], seq), F32),
                   jax.ShapeDtypeStruct((n, CONV_CH), F32)],
        compiler_params=_params(("parallel",)),
        name="proj0",
    )(h, g, wqt, wkv, wvt, wgt, wglu)


def _compress_kernel(x_ref, pe_ref, w1_ref, b1_ref, w2_ref, b2_ref, o_ref):
    nrow = o_ref.shape[2]
    top = jnp.zeros((nrow, w1_ref.shape[3]), F32)
    bot = jnp.zeros((nrow, w1_ref.shape[3]), F32)
    for l in range(CMP_STRIDE):
        x = x_ref[0, pl.ds(l, nrow, stride=CMP_STRIDE), :]
        top = top + _dot((x + pe_ref[0, l]).astype(BF16), w1_ref[0, l])
        bot = bot + _dot((x + pe_ref[0, CMP_STRIDE + l]).astype(BF16), w1_ref[0, CMP_STRIDE + l])
    pre = top + pltpu.roll(bot, nrow - 1, 0) + b1_ref[0]
    hid = _gelu_tanh(pre).astype(BF16)
    o_ref[0, 0] = (_dot(hid, w2_ref[0]) + b2_ref[0]).astype(o_ref.dtype)


def _compress(kcv, pe, w1, b1, w2, b2):
    b, s, _ = kcv.shape
    nr = s // CMP_STRIDE
    return pl.pallas_call(
        _compress_kernel,
        grid=(2, b),
        in_specs=[pl.BlockSpec((1, s, LANES), lambda j, i: (i, 0, j)),
                  pl.BlockSpec((1,) + pe.shape[1:], lambda j, i: (j, 0, 0, 0)),
                  pl.BlockSpec((1,) + w1.shape[1:], lambda j, i: (j, 0, 0, 0)),
                  pl.BlockSpec((1,) + b1.shape[1:], lambda j, i: (j, 0, 0)),
                  pl.BlockSpec((1,) + w2.shape[1:], lambda j, i: (j, 0, 0)),
                  pl.BlockSpec((1,) + b2.shape[1:], lambda j, i: (j, 0, 0))],
        out_specs=pl.BlockSpec((1, 1, nr, LANES), lambda j, i: (j, i, 0, 0)),
        out_shape=jax.ShapeDtypeStruct((2, b, nr, LANES), BF16),
        compiler_params=_params(("parallel", "parallel")),
        name="compress",
    )(kcv, pe, w1, b1, w2, b2)


def _compress_weights(cmp_pe, cmp_w1, cmp_b1, cmp_w2, cmp_b2):
    eye = jnp.eye(N_KV, dtype=F32)
    w1 = cmp_w1.reshape(2, CMP_LEN, HEAD_DIM, CMP_HIDDEN)
    w1 = jnp.einsum('gk,jldh->jlgdkh', eye, w1).reshape(2, CMP_LEN, N_KV * HEAD_DIM, N_KV * CMP_HIDDEN)
    w2 = jnp.einsum('gk,jhd->jghkd', eye, cmp_w2).reshape(2, N_KV * CMP_HIDDEN, N_KV * HEAD_DIM)
    pe = jnp.tile(cmp_pe.reshape(2, CMP_LEN, 1, HEAD_DIM), (1, 1, 1, N_KV)).astype(F32)
    b1 = jnp.tile(cmp_b1.reshape(2, 1, CMP_HIDDEN), (1, 1, N_KV)).astype(F32)
    b2 = jnp.tile(cmp_b2.reshape(2, 1, HEAD_DIM), (1, 1, N_KV)).astype(F32)
    return pe, w1.astype(BF16), b1, w2.astype(BF16), b2


def _col_reduce(x, op):
    rows = x.shape[0]
    part = op(x.reshape(4, rows // 4, x.shape[1]), axis=0)
    return op(part, axis=0, keepdims=True)


def _attn_kernel(qt_ref, kc_ref, vct_ref, ks_ref, vst_ref, kw_ref, vwt_ref, gt_ref,
                 ovt_ref, et_ref, o_ref, rhs_ref, m_ref, l_ref, acc_ref):
    qi = pl.program_id(1)
    q0 = qi * TQ
    nsb = SEL_LEN
    ncp = kc_ref.shape[1]
    wkeys = WINDOW + TQ
    groups = range(N_KV)
    tile4 = lambda a: jnp.concatenate([a] * HPG, axis=1)
    qgs = [jnp.concatenate(
        [qt_ref[0, LANES * (HPG * g + h):LANES * (HPG * g + h + 1), :] for h in range(HPG)], axis=1)
        for g in groups]

    n_idx = lax.broadcasted_iota(jnp.int32, (ncp, TQ), 0)
    t_idx = q0 + lax.broadcasted_iota(jnp.int32, (ncp, TQ), 1)
    cmask = tile4(n_idx * CMP_STRIDE + (CMP_LEN - 1) <= t_idx)
    o_c, imp_t = [], []
    for g in groups:
        s = jnp.where(cmask, _dot(kc_ref[0], qgs[g]), NEG)
        m = _col_reduce(s, jnp.max)
        p = jnp.where(cmask, jnp.exp2(s - m), 0.0)
        l = _col_reduce(p, jnp.sum)
        p = p / jnp.where(l > 0.0, l, 1.0)
        o_c.append(_dot(vct_ref[0], p.astype(BF16)))
        psum = p[:, 0:TQ] + p[:, TQ:2 * TQ] + p[:, 2 * TQ:3 * TQ] + p[:, 3 * TQ:4 * TQ]
        imp_t.append(_dot(ovt_ref[...], psum.astype(BF16))[:nsb])

    j_idx = lax.broadcasted_iota(jnp.int32, (nsb, TQ), 0)
    cur = (q0 + lax.broadcasted_iota(jnp.int32, (nsb, TQ), 1)) // SEL_LEN
    valid = j_idx <= cur
    forced = (j_idx == 0) | (j_idx == cur) | (j_idx == cur - 1)
    sub = lax.broadcasted_iota(jnp.int32, (8, TQ), 0)
    nch = nsb // 8
    for g in groups:
        score = jnp.where(valid, imp_t[g] + jnp.where(forced, 1e6, 0.0), -1e9)
        chunks = [score[8 * v:8 * v + 8] for v in range(nch)]
        counts = [jnp.zeros((8, TQ), F32) for _ in range(nch)]
        for i in range(nsb):
            row = jnp.broadcast_to(score[i:i + 1], (8, TQ))
            for v in range(nch):
                if i < 8 * v:
                    ahead = row >= chunks[v]
                elif i >= 8 * v + 8:
                    ahead = row > chunks[v]
                else:
                    ahead = jnp.where(sub > (i - 8 * v),
                                      jnp.where(row >= chunks[v], 1.0, 0.0),
                                      jnp.where(row > chunks[v], 1.0, 0.0)) > 0.5
                counts[v] = counts[v] + jnp.where(ahead, 1.0, 0.0)
        bias_t = jnp.concatenate(
            [jnp.where((c < float(N_SEL)) & valid[8 * v:8 * v + 8], 0.0, NEG) for v, c in enumerate(counts)]
            + [jnp.zeros((LANES - nsb, TQ), F32)], axis=0).astype(BF16)
        rhs_ref[g] = jnp.concatenate([qgs[g], tile4(bias_t)], axis=0)

    m_ref[...] = jnp.full(m_ref.shape, NEG, F32)
    l_ref[...] = jnp.zeros(l_ref.shape, F32)
    acc_ref[...] = jnp.zeros(acc_ref.shape, F32)

    def sel_tile(kt, diagonal):
        k0 = pl.multiple_of(kt * TK_SEL, TK_SEL)
        lhs = jnp.concatenate([ks_ref[0, pl.ds(k0, TK_SEL), :], et_ref[pl.ds(k0, TK_SEL), :]], axis=1)
        vt = vst_ref[0, :, pl.ds(k0, TK_SEL)]
        if diagonal:
            kpos = k0 + lax.broadcasted_iota(jnp.int32, (TK_SEL, TQ), 0)
            tpos = q0 + lax.broadcasted_iota(jnp.int32, (TK_SEL, TQ), 1)
            causal = tile4(kpos <= tpos)
        for g in groups:
            sc = _dot(lhs, rhs_ref[g])
            if diagonal:
                sc = jnp.where(causal, sc, NEG)
            m_i = m_ref[g]
            m_new = jnp.maximum(m_i, _col_reduce(sc, jnp.max))
            alpha = jnp.exp2(m_i - m_new)
            pexp = jnp.exp2(sc - m_new)
            l_ref[g] = alpha * l_ref[g] + _col_reduce(pexp, jnp.sum)
            acc_ref[g] = alpha * acc_ref[g] + _dot(vt, pexp.astype(BF16))
            m_ref[g] = m_new

    last = q0 // TK_SEL

    def sel_body(kt, carry):
        sel_tile(kt, False)
        return carry

    lax.fori_loop(0, last, sel_body, 0)
    sel_tile(last, True)

    k0 = pl.multiple_of(jnp.maximum(q0 - WINDOW, 0), TQ)
    kpos = k0 + lax.broadcasted_iota(jnp.int32, (wkeys, TQ), 0)
    tpos = q0 + lax.broadcasted_iota(jnp.int32, (wkeys, TQ), 1)
    wmask = tile4((kpos <= tpos) & (kpos > tpos - WINDOW))
    kwin = kw_ref[0, pl.ds(k0, wkeys), :]
    vwin = vwt_ref[0, :, pl.ds(k0, wkeys)]
    o_w = []
    for g in groups:
        sc = jnp.where(wmask, _dot(kwin, qgs[g]), NEG)
        pexp = jnp.exp2(sc - _col_reduce(sc, jnp.max))
        o_w.append(_dot(vwin, pexp.astype(BF16)) / _col_reduce(pexp, jnp.sum))

    blocks = []
    for g in groups:
        o_s = acc_ref[g] / l_ref[g]
        for h in range(HPG):
            r0 = LANES * g + 3 * h
            cols = slice(h * TQ, (h + 1) * TQ)
            og = (gt_ref[0, r0:r0 + 1, :] * o_c[g][:, cols] + gt_ref[0, r0 + 1:r0 + 2, :] * o_s[:, cols]
                  + gt_ref[0, r0 + 2:r0 + 3, :] * o_w[g][:, cols])
            blocks.append(og[HEAD_DIM * g:HEAD_DIM * (g + 1)])
    o_ref[0] = jnp.concatenate(blocks, axis=0).T.astype(o_ref.dtype)


def _attention(qt, kc, vct, kv, vt, gt, ovt, et):
    b, _, s = qt.shape
    ncp = kc.shape[1]
    kspec = lambda j: pl.BlockSpec((1, s, LANES), lambda bi, qi: (bi, 0, j))
    vspec = lambda j: pl.BlockSpec((1, LANES, s), lambda bi, qi: (bi, j, 0))
    return pl.pallas_call(
        _attn_kernel,
        grid=(b, s // TQ),
        in_specs=[pl.BlockSpec((1, N_HEADS * LANES, TQ), lambda bi, qi: (bi, 0, qi)),
                  pl.BlockSpec((1, ncp, LANES), lambda bi, qi: (bi, 0, 0)),
                  pl.BlockSpec((1, LANES, ncp), lambda bi, qi: (bi, 0, 0)),
                  kspec(2), vspec(0), kspec(4), vspec(1),
                  pl.BlockSpec((1, N_KV * LANES, TQ), lambda bi, qi: (bi, 0, qi)),
                  pl.BlockSpec(ovt.shape, lambda bi, qi: (0, 0)),
                  pl.BlockSpec(et.shape, lambda bi, qi: (0, 0))],
        out_specs=pl.BlockSpec((1, TQ, N_HEADS * HEAD_DIM), lambda bi, qi: (bi, qi, 0)),
        out_shape=jax.ShapeDtypeStruct((b, s, N_HEADS * HEAD_DIM), BF16),
        scratch_shapes=[pltpu.VMEM((N_KV, 2 * LANES, HPG * TQ), BF16),
                        pltpu.VMEM((N_KV, 1, HPG * TQ), F32), pltpu.VMEM((N_KV, 1, HPG * TQ), F32),
                        pltpu.VMEM((N_KV, LANES, HPG * TQ), F32)],
        compiler_params=_params(("parallel", "arbitrary")),
        name="nsa_attention",
    )(qt, kc, vct, kv, vt, kv, vt, gt, ovt, et)


def _conv_kernel(c_ref, halo_ref, cw_ref, cb_ref, g_ref, b_ref, o_ref, xs_ref, sh_ref):
    i = pl.program_id(1)
    ts = c_ref.shape[1]
    xs_ref[0:CONV_HALO, :] = jnp.where(i > 0, halo_ref[0], 0.0)
    xs_ref[CONV_HALO:CONV_HALO + ts, :] = c_ref[0]
    first = CONV_HALO - (CONV_WIDTH - 1)
    acc = jnp.zeros((ts, CONV_CH), F32)
    for r in range(8):
        offs = [off for off in range(first, first + CONV_WIDTH) if off % 8 == r]
        span = max(offs) - r
        sh_ref[r % 2, 0:span + ts, :] = xs_ref[r:r + span + ts, :]
        for off in offs:
            w = off - first
            acc = acc + sh_ref[r % 2, off - r:off - r + ts, :] * cw_ref[w:w + 1, :]
    y = _layer_norm(acc + cb_ref[...], g_ref[...], b_ref[...])
    o_ref[0] = _silu(y).astype(o_ref.dtype)


def _conv_branch(c, cw, cb, ln_g, ln_b):
    b, s, ch = c.shape
    ts = TS_CONV
    per = ts // CONV_HALO
    vec = pl.BlockSpec((1, ch), lambda bi, i: (0, 0))
    return pl.pallas_call(
        _conv_kernel,
        grid=(b, s // ts),
        in_specs=[pl.BlockSpec((1, ts, ch), lambda bi, i: (bi, i, 0)),
                  pl.BlockSpec((1, CONV_HALO, ch), lambda bi, i: (bi, jnp.maximum(i * per - 1, 0), 0)),
                  pl.BlockSpec((CONV_WIDTH, ch), lambda bi, i: (0, 0)), vec, vec, vec],
        out_specs=pl.BlockSpec((1, ts, ch), lambda bi, i: (bi, i, 0)),
        out_shape=jax.ShapeDtypeStruct((b, s, ch), BF16),
        scratch_shapes=[pltpu.VMEM((CONV_HALO + ts, ch), F32), pltpu.VMEM((2, CONV_HALO + ts, ch), F32)],
        compiler_params=_params(("parallel", "arbitrary")),
        name="conv_branch",
    )(c, c, cw, cb, ln_g, ln_b)


def _outproj_kernel(oa_ref, cc_ref, wt_ref, wb_ref, h_ref, gpost_ref, gnext_ref, h1_ref, u_ref):
    m = _dot(oa_ref[...], wt_ref[...]) + _dot(cc_ref[...], wb_ref[...])
    h1 = h_ref[...] + _rms(m, gpost_ref[...])
    h1_ref[...] = h1
    u_ref[...] = _rms(h1, gnext_ref[...]).astype(u_ref.dtype)


def _outproj(oa, cc, wt, wb, h, gpost, gnext):
    n, d = h.shape
    tm = TM_PROJ
    full = lambda a: pl.BlockSpec(a.shape, lambda i: (0,) * a.ndim)
    row = lambda c: pl.BlockSpec((tm, c), lambda i: (i, 0))
    return pl.pallas_call(
        _outproj_kernel,
        grid=(n // tm,),
        in_specs=[row(oa.shape[1]), row(cc.shape[1]), full(wt), full(wb), row(d), full(gpost), full(gnext)],
        out_specs=[row(d), row(d)],
        out_shape=[jax.ShapeDtypeStruct((n, d), F32), jax.ShapeDtypeStruct((n, d), BF16)],
        compiler_params=_params(("parallel",)),
        name="outproj",
    )(oa, cc, wt, wb, h, gpost, gnext)


def _ffn_kernel(u_ref, wg_ref, wu_ref, wd_ref, h_ref, gpost_ref, gnext_ref, h2_ref, u2_ref, acc_ref):
    f = pl.program_id(1)

    @pl.when(f == 0)
    def _():
        acc_ref[...] = jnp.zeros_like(acc_ref)

    u = u_ref[...]
    act = (_silu(_dot(u, wg_ref[...])) * _dot(u, wu_ref[...])).astype(BF16)
    acc_ref[...] += _dot(act, wd_ref[...])

    @pl.when(f == pl.num_programs(1) - 1)
    def _():
        h2 = h_ref[...] + _rms(acc_ref[...], gpost_ref[...])
        h2_ref[...] = h2
        u2_ref[...] = _rms(h2, gnext_ref[...]).astype(u2_ref.dtype)


def _ffn(u, wg, wu, wd, h, gpost, gnext):
    n, d = h.shape
    ff = wg.shape[1]
    tm, tf = TM_FFN, TF_FFN
    row = pl.BlockSpec((tm, d), lambda i, f: (i, 0))
    vec = pl.BlockSpec((1, d), lambda i, f: (0, 0))
    return pl.pallas_call(
        _ffn_kernel,
        grid=(n // tm, ff // tf),
        in_specs=[row, pl.BlockSpec((d, tf), lambda i, f: (0, f)), pl.BlockSpec((d, tf), lambda i, f: (0, f)),
                  pl.BlockSpec((tf, d), lambda i, f: (f, 0)), row, vec, vec],
        out_specs=[row, row],
        out_shape=[jax.ShapeDtypeStruct((n, d), F32), jax.ShapeDtypeStruct((n, d), BF16)],
        scratch_shapes=[pltpu.VMEM((tm, d), F32)],
        compiler_params=_params(("parallel", "arbitrary")),
        name="dense_ffn",
    )(u, wg, wu, wd, h, gpost, gnext)


def _gmlp_kernel(u_ref, win_ref, lng_ref, lnb_ref, ws_ref, bs_ref, wout_ref, h_ref, gpost_ref, gnext_ref,
                 wrh_ref, wrl_ref, h3_ref, u4_ref, route_ref, mix_ref):
    tm = u_ref.shape[0]
    width = wout_ref.shape[0]
    z = _gelu_tanh(_dot(u_ref[...], win_ref[...]))
    z1 = z[:, :width]
    z2 = _layer_norm(z[:, width:], lng_ref[...], lnb_ref[...]).astype(BF16)
    r_idx = lax.broadcasted_iota(jnp.int32, (GMLP_CHUNK, GMLP_CHUNK), 0)
    c_idx = lax.broadcasted_iota(jnp.int32, (GMLP_CHUNK, GMLP_CHUNK), 1)
    gw = width // GMLP_GROUPS
    for g in range(GMLP_GROUPS):
        wsg = jnp.where(c_idx <= r_idx, ws_ref[g], 0.0).astype(BF16)
        for c in range(tm // GMLP_CHUNK):
            rows = slice(c * GMLP_CHUNK, (c + 1) * GMLP_CHUNK)
            cols = slice(g * gw, (g + 1) * gw)
            mix_ref[rows, cols] = _dot(wsg, z2[rows, cols]) + bs_ref[:, cols]
    gated = (z1 * mix_ref[...]).astype(BF16)
    y = _dot(gated, wout_ref[...])
    h3 = h_ref[...] + _rms(y, gpost_ref[...])
    h3_ref[...] = h3
    u4 = _rms(h3, gnext_ref[...])
    u4_ref[...] = u4

    u_hi = u4.astype(BF16)
    u_lo = (u4 - u_hi.astype(F32)).astype(BF16)
    logits = _dot(u_hi, wrh_ref[...]) + (_dot(u_lo, wrh_ref[...]) + _dot(u_hi, wrl_ref[...]))
    lane = lax.broadcasted_iota(jnp.int32, (tm, LANES), 1).astype(F32)
    lg = jnp.where(lane < float(N_EXPERTS), logits, NEG)
    m1 = jnp.max(lg, axis=-1, keepdims=True)
    i1 = jnp.min(jnp.where(lg == m1, lane, float(LANES)), axis=-1, keepdims=True)
    lg2 = jnp.where(lane == i1, NEG, lg)
    m2 = jnp.max(lg2, axis=-1, keepdims=True)
    i2 = jnp.min(jnp.where(lg2 == m2, lane, float(LANES)), axis=-1, keepdims=True)
    e2 = jnp.exp(m2 - m1)
    den = 1.0 + e2
    route_ref[...] = jnp.where(lane == 0.0, i1, jnp.where(lane == 1.0, i2,
                               jnp.where(lane == 2.0, 1.0 / den, jnp.where(lane == 3.0, e2 / den, 0.0))))


def _gmlp(u, win, lng, lnb, ws, bsb, wout, h, gpost, gnext, wrh, wrl):
    n, d = h.shape
    tm = TM_PROJ
    full = lambda a: pl.BlockSpec(a.shape, lambda i: (0,) * a.ndim)
    row = lambda c: pl.BlockSpec((tm, c), lambda i: (i, 0))
    return pl.pallas_call(
        _gmlp_kernel,
        grid=(n // tm,),
        in_specs=[row(d), full(win), full(lng), full(lnb), full(ws), full(bsb), full(wout), row(d),
                  full(gpost), full(gnext), full(wrh), full(wrl)],
        out_specs=[row(d), row(d), row(LANES)],
        out_shape=[jax.ShapeDtypeStruct((n, d), F32), jax.ShapeDtypeStruct((n, d), F32),
                   jax.ShapeDtypeStruct((n, LANES), F32)],
        scratch_shapes=[pltpu.VMEM((tm, wout.shape[0]), F32)],
        compiler_params=_params(("parallel",)),
        name="gmlp_router",
    )(u, win, lng, lnb, ws, bsb, wout, h, gpost, gnext, wrh, wrl)


def _expert_kernel(tile_e_ref, nused_ref, rowtok_ref, x_hbm, wg_ref, wu_ref, wd_ref, y_ref,
                   xg_ref, xb_ref, acc_ref, sem, *, nf):
    i = pl.program_id(0)
    f = pl.program_id(1)
    tm = xg_ref.shape[1]
    nused = nused_ref[0]
    active = i < nused
    slot = i % 2

    def row_copy(tok, s, r):
        return pltpu.make_async_copy(x_hbm.at[pl.ds(tok, 1)], xg_ref.at[s, pl.ds(r, 1)], sem.at[s])

    def wait_tile(s):
        pltpu.make_async_copy(x_hbm.at[pl.ds(0, tm)], xg_ref.at[s], sem.at[s]).wait()

    @pl.when((i == 0) & (f == 0))
    def _():
        def body(r, carry):
            row_copy(rowtok_ref[r], 0, r).start()
            return carry

        lax.fori_loop(0, tm, body, 0, unroll=8)

    @pl.when((i <= nused) & (f == 0))
    def _():
        wait_tile(slot)

    @pl.when(active & (f == 0))
    def _():
        xb_ref[...] = xg_ref[slot].astype(BF16)
        acc_ref[...] = jnp.zeros_like(acc_ref)

    @pl.when(active)
    def _():
        per = tm // nf
        base = (i + 1) * tm + f * per
        for j in range(per):
            row_copy(rowtok_ref[base + j], 1 - slot, f * per + j).start()
        x = xb_ref[...]
        act = (_silu(_dot(x, wg_ref[0])) * _dot(x, wu_ref[0])).astype(BF16)
        acc_ref[...] += _dot(act, wd_ref[0])

    last = f == pl.num_programs(1) - 1

    @pl.when(active & last)
    def _():
        y_ref[...] = acc_ref[...]

    @pl.when(jnp.logical_not(active) & last)
    def _():
        y_ref[...] = jnp.zeros_like(y_ref)


def _experts(tile_e, nused, rowtok, x, wg, wu, wd):
    n_rows = rowtok.shape[0]
    d = x.shape[1]
    ff = wg.shape[2]
    tm, tf = TM_MOE, TF_MOE
    nf = ff // tf

    def fidx(i, f, te, nu):
        return jnp.where(i < nu[0], f, nf - 1)

    grid_spec = pltpu.PrefetchScalarGridSpec(
        num_scalar_prefetch=3,
        grid=(n_rows // tm, nf),
        in_specs=[pl.BlockSpec(memory_space=pl.ANY),
                  pl.BlockSpec((1, d, tf), lambda i, f, te, nu, rt: (te[i], 0, fidx(i, f, te, nu))),
                  pl.BlockSpec((1, d, tf), lambda i, f, te, nu, rt: (te[i], 0, fidx(i, f, te, nu))),
                  pl.BlockSpec((1, tf, d), lambda i, f, te, nu, rt: (te[i], fidx(i, f, te, nu), 0))],
        out_specs=pl.BlockSpec((tm, d), lambda i, f, te, nu, rt: (i, 0)),
        scratch_shapes=[pltpu.VMEM((2, tm, d), F32), pltpu.VMEM((tm, d), BF16), pltpu.VMEM((tm, d), F32),
                        pltpu.SemaphoreType.DMA((2,))],
    )
    return pl.pallas_call(
        functools.partial(_expert_kernel, nf=nf),
        grid_spec=grid_spec,
        out_shape=jax.ShapeDtypeStruct((n_rows, d), F32),
        compiler_params=_params(("arbitrary", "arbitrary")),
        name="moe_experts",
    )(tile_e, nused, rowtok, x, wg, wu, wd)


def _combine_kernel(dest_ref, y_hbm, route_ref, h_ref, gpost_ref, o_ref, ya_ref, yb_ref, sem):
    i = pl.program_id(0)
    tc = ya_ref.shape[1]
    slot = i % 2

    def copies(d0, d1, s, r):
        return (pltpu.make_async_copy(y_hbm.at[pl.ds(d0, 1)], ya_ref.at[s, pl.ds(r, 1)], sem.at[0, s]),
                pltpu.make_async_copy(y_hbm.at[pl.ds(d1, 1)], yb_ref.at[s, pl.ds(r, 1)], sem.at[1, s]))

    def issue_tile(t, s):
        for r in range(tc):
            p = 2 * (t * tc + r)
            ca, cb = copies(dest_ref[p], dest_ref[p + 1], s, r)
            ca.start()
            cb.start()

    def wait_tile(s):
        pltpu.make_async_copy(y_hbm.at[pl.ds(0, tc)], ya_ref.at[s], sem.at[0, s]).wait()
        pltpu.make_async_copy(y_hbm.at[pl.ds(0, tc)], yb_ref.at[s], sem.at[1, s]).wait()

    @pl.when(i == 0)
    def _():
        issue_tile(0, 0)

    @pl.when(i + 1 < pl.num_programs(0))
    def _():
        issue_tile(i + 1, 1 - slot)

    wait_tile(slot)
    route = route_ref[...]
    moe = route[:, 2:3] * ya_ref[slot] + route[:, 3:4] * yb_ref[slot]
    o_ref[...] = h_ref[...] + _rms(moe, gpost_ref[...])


def _combine(dest, y, route, h, gpost):
    n, d = h.shape
    tc = TC_COMB
    grid_spec = pltpu.PrefetchScalarGridSpec(
        num_scalar_prefetch=1,
        grid=(n // tc,),
        in_specs=[pl.BlockSpec(memory_space=pl.ANY),
                  pl.BlockSpec((tc, LANES), lambda i, ds: (i, 0)),
                  pl.BlockSpec((tc, d), lambda i, ds: (i, 0)),
                  pl.BlockSpec((1, d), lambda i, ds: (0, 0))],
        out_specs=pl.BlockSpec((tc, d), lambda i, ds: (i, 0)),
        scratch_shapes=[pltpu.VMEM((2, tc, d), F32), pltpu.VMEM((2, tc, d), F32),
                        pltpu.SemaphoreType.DMA((2, 2))],
    )
    return pl.pallas_call(
        _combine_kernel,
        grid_spec=grid_spec,
        out_shape=jax.ShapeDtypeStruct((n, d), F32),
        compiler_params=_params(("arbitrary",)),
        name="moe_combine",
    )(dest, y, route, h, gpost)


def _overlap_table_t(seq):
    nc = (seq - CMP_LEN) // CMP_STRIDE + 1
    ncp = seq // CMP_STRIDE
    nsb = seq // SEL_LEN
    cs = np.arange(ncp)[None, :] * CMP_STRIDE
    js = np.arange(LANES)[:, None] * SEL_LEN
    ov = (cs < js + SEL_LEN) & (cs + CMP_LEN > js) & (np.arange(ncp)[None, :] < nc) & (np.arange(LANES)[:, None] < nsb)
    return jnp.asarray(ov.astype(np.float32), BF16)


def _expand_table_t(seq):
    e = (np.arange(seq)[:, None] // SEL_LEN) == np.arange(LANES)[None, :]
    return jnp.asarray(e.astype(np.float32), BF16)


def _layer0_weights(w_in):
    d = w_in.shape[0]
    qc = N_HEADS * HEAD_DIM
    kvc = 6 * N_KV * HEAD_DIM
    gc = 3 * N_HEADS
    gh = N_KV * HEAD_DIM
    wq = (w_in[:, :qc] * (HEAD_DIM ** -0.5 * np.log2(np.e))).reshape(d, N_KV, HPG, HEAD_DIM)
    zeros = jnp.zeros_like(wq[:, 0])
    wq_pad = jnp.stack([jnp.concatenate([wq[:, 0], zeros], axis=-1),
                        jnp.concatenate([zeros, wq[:, 1]], axis=-1)], axis=1)
    wqt = wq_pad.reshape(d, N_HEADS * LANES).T.astype(BF16)
    wkv = w_in[:, qc:qc + kvc]
    wvt = jnp.concatenate([wkv[:, 3 * gh:4 * gh], wkv[:, 5 * gh:6 * gh]], axis=1).T.astype(BF16)
    wg = w_in[:, qc + kvc:qc + kvc + gc].reshape(d, N_KV, 3 * HPG)
    wgt = jnp.pad(wg, ((0, 0), (0, 0), (0, LANES - 3 * HPG))).reshape(d, N_KV * LANES).T.astype(BF16)
    wglu = w_in[:, qc + kvc + gc:].astype(BF16)
    return wqt, wkv.astype(BF16), wvt, wgt, wglu


def _dispatch_plan(route, n_tok):
    tm = TM_MOE
    e_flat = route[:, 0:2].astype(jnp.int32).reshape(-1)
    onehot = (e_flat[:, None] == jnp.arange(N_EXPERTS, dtype=jnp.int32)[None, :]).astype(jnp.int32)
    rank = jnp.sum((jnp.cumsum(onehot, axis=0) - onehot) * onehot, axis=1)
    counts = jnp.sum(onehot, axis=0)
    padded = (counts + tm - 1) // tm * tm
    pad_ends = jnp.cumsum(padded)
    pad_starts = pad_ends - padded
    dest = (pad_starts[e_flat] + rank).astype(jnp.int32)
    n_rows = 2 * n_tok + (N_EXPERTS + 1) * tm
    n_tiles = n_rows // tm
    rowtok = jnp.zeros((n_rows,), jnp.int32).at[dest].set(jnp.arange(2 * n_tok, dtype=jnp.int32) // 2)
    nused = (pad_ends[-1] // tm).astype(jnp.int32)
    tiles = jnp.minimum(jnp.arange(n_tiles, dtype=jnp.int32), nused - 1) * tm
    tile_e = jnp.minimum(jnp.sum((pad_ends[None, :] <= tiles[:, None]).astype(jnp.int32), axis=1), N_EXPERTS - 1)
    return tile_e, nused.reshape(1), rowtok, dest


def kernel(x, norm_mix_pre, norm_mix_post, norm_ffn_pre, norm_ffn_post, nsa_conv_w_in, cmp_pe, cmp_w1, cmp_b1, cmp_w2, cmp_b2, conv_w, conv_b, conv_ln_g, conv_ln_b, nsa_conv_w_out, ffn_w_gate, ffn_w_up, ffn_w_down, gmlp_w_in, gmlp_ln_g, gmlp_ln_b, gmlp_w_s, gmlp_b_s, gmlp_w_out, moe_w_router, moe_w_gate, moe_w_up, moe_w_down):
    b, s, d = x.shape
    n = b * s
    assert d == D_MODEL and s // SEL_LEN == SEL_LEN and s % (2 * TK_SEL) == 0 and n % TM_PROJ == 0
    h0 = x.reshape(n, d)
    vec = lambda a: a.reshape(1, -1).astype(F32)

    wqt, wkv, wvt, wgt, wglu = _layer0_weights(nsa_conv_w_in[0])
    qt, kv, kcv, vt, gt, c = _proj0(h0, vec(norm_mix_pre[0]), wqt, wkv, wvt, wgt, wglu, b)
    kv3 = kv.reshape(b, s, 6 * N_KV * HEAD_DIM)
    cmp = _compress(kcv.reshape(b, s, 2 * LANES), *_compress_weights(cmp_pe[0], cmp_w1[0], cmp_b1[0],
                                                                     cmp_w2[0], cmp_b2[0]))
    kc = cmp[0]
    vct = cmp[1].transpose(0, 2, 1)

    o_attn = _attention(qt, kc, vct, kv3, vt, gt, _overlap_table_t(s), _expand_table_t(s))
    cc = _conv_branch(c.reshape(b, s, CONV_CH), conv_w[0].reshape(CONV_WIDTH, CONV_CH).astype(F32),
                      vec(conv_b[0]), vec(conv_ln_g[0]), vec(conv_ln_b[0]))
    w_out = nsa_conv_w_out[0].astype(BF16)
    nw = N_HEADS * HEAD_DIM
    h1, u1 = _outproj(o_attn.reshape(n, nw), cc.reshape(n, CONV_CH), w_out[:nw], w_out[nw:], h0,
                      vec(norm_mix_post[0]), vec(norm_ffn_pre[0]))
    h2, u2 = _ffn(u1, ffn_w_gate[0].astype(BF16), ffn_w_up[0].astype(BF16), ffn_w_down[0].astype(BF16), h1,
                  vec(norm_ffn_post[0]), vec(norm_mix_pre[1]))

    bsb = jnp.repeat(gmlp_b_s[0].T.astype(F32), d // GMLP_GROUPS, axis=1)
    wr = jnp.pad(moe_w_router[0].astype(F32), ((0, 0), (0, LANES - N_EXPERTS)))
    wr_hi = wr.astype(BF16)
    wr_lo = (wr - wr_hi.astype(F32)).astype(BF16)
    h3, u4, route = _gmlp(u2, gmlp_w_in[0].astype(BF16), vec(gmlp_ln_g[0]), vec(gmlp_ln_b[0]),
                          gmlp_w_s[0].astype(F32), bsb, gmlp_w_out[0].astype(BF16), h2,
                          vec(norm_mix_post[1]), vec(norm_ffn_pre[1]), wr_hi, wr_lo)
    tile_e, nused, rowtok, dest = _dispatch_plan(route, n)
    y = _experts(tile_e, nused, rowtok, u4, moe_w_gate[0].astype(BF16), moe_w_up[0].astype(BF16),
                 moe_w_down[0].astype(BF16))
    h4 = _combine(dest, y, route, h3, vec(norm_ffn_post[1]))
    return h4.reshape(b, s, d)
```

```python
import functools

import numpy as np
import jax
import jax.numpy as jnp
from jax import lax
from jax.experimental import pallas as pl
from jax.experimental.pallas import tpu as pltpu

F32 = jnp.float32
BF16 = jnp.bfloat16

D_MODEL = 1024
N_HEADS = 8
N_KV = 2
HPG = N_HEADS // N_KV
HEAD_DIM = 64
CMP_LEN = 32
CMP_STRIDE = 16
CMP_HIDDEN = 256
SEL_LEN = 64
N_SEL = 16
WINDOW = 512
CONV_CH = D_MODEL // 2
CONV_WIDTH = 31
GMLP_GROUPS = 8
GMLP_CHUNK = 128
N_EXPERTS = 8
EPS = 1e-6
NEG = -1e30

LANES = 128
TQ = 256
TK_SEL = 1024
TM_PROJ = 512
TS_CONV = 512
CONV_HALO = 32
TM_FFN = 256
TF_FFN = 2816
TM_MOE = 512
TF_MOE = 1792
TC_COMB = 256
VMEM_LIMIT = 56 * 1024 * 1024


def _params(sem):
    return pltpu.CompilerParams(dimension_semantics=sem, vmem_limit_bytes=VMEM_LIMIT)


def _dot(a, b):
    return jnp.dot(a, b, preferred_element_type=F32)


def _dot_nt(a, b):
    return lax.dot_general(a, b, (((1,), (1,)), ((), ())), preferred_element_type=F32)


def _rms(x, g):
    return x * lax.rsqrt(jnp.mean(x * x, axis=-1, keepdims=True) + EPS) * g


def _layer_norm(x, g, b):
    mu = jnp.mean(x, axis=-1, keepdims=True)
    xc = x - mu
    var = jnp.mean(xc * xc, axis=-1, keepdims=True)
    return xc * lax.rsqrt(var + EPS) * g + b


def _sigmoid(x):
    return 1.0 / (1.0 + jnp.exp(-x))


def _silu(x):
    return x * _sigmoid(x)


def _gelu_tanh(x):
    c = np.float32(np.sqrt(2.0 / np.pi))
    return 0.5 * x * (1.0 + jnp.tanh(c * (x + 0.044715 * (x * x * x))))


def _proj0_kernel(h_ref, g_ref, wqt_ref, wkv_ref, wvt_ref, wgt_ref, wglu_ref,
                  qt_ref, kv_ref, kcv_ref, vt_ref, gt_ref, c_ref):
    u = _rms(h_ref[...], g_ref[...]).astype(BF16)
    qt_ref[0] = _dot_nt(wqt_ref[...], u).astype(BF16)
    kv = _dot(u, wkv_ref[...])
    kv_ref[...] = kv.astype(BF16)
    kcv_ref[...] = kv[:, :kcv_ref.shape[1]]
    vt_ref[0] = _dot_nt(wvt_ref[...], u).astype(BF16)
    gt_ref[0] = _sigmoid(_dot_nt(wgt_ref[...], u))
    glu = _dot(u, wglu_ref[...])
    c_ref[...] = glu[:, :CONV_CH] * _sigmoid(glu[:, CONV_CH:])


def _proj0(h, g, wqt, wkv, wvt, wgt, wglu, batch):
    n, d = h.shape
    tm = TM_PROJ
    seq = n // batch
    per = seq // tm
    full = lambda a: pl.BlockSpec(a.shape, lambda i: (0,) * a.ndim)
    row = lambda c: pl.BlockSpec((tm, c), lambda i: (i, 0))
    col = lambda r: pl.BlockSpec((1, r, tm), lambda i: (i // per, 0, i % per))
    return pl.pallas_call(
        _proj0_kernel,
        grid=(n // tm,),
        in_specs=[row(d), full(g), full(wqt), full(wkv), full(wvt), full(wgt), full(wglu)],
        out_specs=[col(wqt.shape[0]), row(wkv.shape[1]), row(2 * LANES), col(wvt.shape[0]), col(wgt.shape[0]),
                   row(CONV_CH)],
        out_shape=[jax.ShapeDtypeStruct((batch, wqt.shape[0], seq), BF16),
                   jax.ShapeDtypeStruct((n, wkv.shape[1]), BF16),
                   jax.ShapeDtypeStruct((n, 2 * LANES), F32),
                   jax.ShapeDtypeStruct((batch, wvt.shape[0], seq), BF16),
                   jax.ShapeDtypeStruct((batch, wgt.shape[0], seq), F32),
                   jax.ShapeDtypeStruct((n, CONV_CH), F32)],
        compiler_params=_params(("parallel",)),
        name="proj0",
    )(h, g, wqt, wkv, wvt, wgt, wglu)


def _compress_kernel(x_ref, pe_ref, w1_ref, b1_ref, w2_ref, b2_ref, o_ref):
    nrow = o_ref.shape[2]
    top = jnp.zeros((nrow, w1_ref.shape[3]), F32)
    bot = jnp.zeros((nrow, w1_ref.shape[3]), F32)
    for l in range(CMP_STRIDE):
        x = x_ref[0, pl.ds(l, nrow, stride=CMP_STRIDE), :]
        top = top + _dot((x + pe_ref[0, l]).astype(BF16), w1_ref[0, l])
        bot = bot + _dot((x + pe_ref[0, CMP_STRIDE + l]).astype(BF16), w1_ref[0, CMP_STRIDE + l])
    pre = top + pltpu.roll(bot, nrow - 1, 0) + b1_ref[0]
    hid = _gelu_tanh(pre).astype(BF16)
    o_ref[0, 0] = (_dot(hid, w2_ref[0]) + b2_ref[0]).astype(o_ref.dtype)


def _compress(kcv, pe, w1, b1, w2, b2):
    b, s, _ = kcv.shape
    nr = s // CMP_STRIDE
    return pl.pallas_call(
        _compress_kernel,
        grid=(2, b),
        in_specs=[pl.BlockSpec((1, s, LANES), lambda j, i: (i, 0, j)),
                  pl.BlockSpec((1,) + pe.shape[1:], lambda j, i: (j, 0, 0, 0)),
                  pl.BlockSpec((1,) + w1.shape[1:], lambda j, i: (j, 0, 0, 0)),
                  pl.BlockSpec((1,) + b1.shape[1:], lambda j, i: (j, 0, 0)),
                  pl.BlockSpec((1,) + w2.shape[1:], lambda j, i: (j, 0, 0)),
                  pl.BlockSpec((1,) + b2.shape[1:], lambda j, i: (j, 0, 0))],
        out_specs=pl.BlockSpec((1, 1, nr, LANES), lambda j, i: (j, i, 0, 0)),
        out_shape=jax.ShapeDtypeStruct((2, b, nr, LANES), BF16),
        compiler_params=_params(("parallel", "parallel")),
        name="compress",
    )(kcv, pe, w1, b1, w2, b2)


def _compress_weights(cmp_pe, cmp_w1, cmp_b1, cmp_w2, cmp_b2):
    eye = jnp.eye(N_KV, dtype=F32)
    w1 = cmp_w1.reshape(2, CMP_LEN, HEAD_DIM, CMP_HIDDEN)
    w1 = jnp.einsum('gk,jldh->jlgdkh', eye, w1).reshape(2, CMP_LEN, N_KV * HEAD_DIM, N_KV * CMP_HIDDEN)
    w2 = jnp.einsum('gk,jhd->jghkd', eye, cmp_w2).reshape(2, N_KV * CMP_HIDDEN, N_KV * HEAD_DIM)
    pe = jnp.tile(cmp_pe.reshape(2, CMP_LEN, 1, HEAD_DIM), (1, 1, 1, N_KV)).astype(F32)
    b1 = jnp.tile(cmp_b1.reshape(2, 1, CMP_HIDDEN), (1, 1, N_KV)).astype(F32)
    b2 = jnp.tile(cmp_b2.reshape(2, 1, HEAD_DIM), (1, 1, N_KV)).astype(F32)
    return pe, w1.astype(BF16), b1, w2.astype(BF16), b2


def _col_reduce(x, op):
    rows = x.shape[0]
    part = op(x.reshape(4, rows // 4, x.shape[1]), axis=0)
    return op(part, axis=0, keepdims=True)


def _attn_kernel(qt_ref, kc_ref, vct_ref, ks_ref, vst_ref, kw_ref, vwt_ref, gt_ref,
                 ovt_ref, et_ref, o_ref, rhs_ref, m_ref, l_ref, acc_ref):
    qi = pl.program_id(1)
    q0 = qi * TQ
    nsb = SEL_LEN
    ncp = kc_ref.shape[1]
    wkeys = WINDOW + TQ
    groups = range(N_KV)
    tile4 = lambda a: jnp.concatenate([a] * HPG, axis=1)
    qgs = [jnp.concatenate(
        [qt_ref[0, LANES * (HPG * g + h):LANES * (HPG * g + h + 1), :] for h in range(HPG)], axis=1)
        for g in groups]

    n_idx = lax.broadcasted_iota(jnp.int32, (ncp, TQ), 0)
    t_idx = q0 + lax.broadcasted_iota(jnp.int32, (ncp, TQ), 1)
    cmask = tile4(n_idx * CMP_STRIDE + (CMP_LEN - 1) <= t_idx)
    o_c, imp_t = [], []
    for g in groups:
        s = jnp.where(cmask, _dot(kc_ref[0], qgs[g]), NEG)
        m = _col_reduce(s, jnp.max)
        p = jnp.where(cmask, jnp.exp2(s - m), 0.0)
        l = _col_reduce(p, jnp.sum)
        p = p / jnp.where(l > 0.0, l, 1.0)
        o_c.append(_dot(vct_ref[0], p.astype(BF16)))
        psum = p[:, 0:TQ] + p[:, TQ:2 * TQ] + p[:, 2 * TQ:3 * TQ] + p[:, 3 * TQ:4 * TQ]
        imp_t.append(_dot(ovt_ref[...], psum.astype(BF16))[:nsb])

    j_idx = lax.broadcasted_iota(jnp.int32, (nsb, TQ), 0)
    cur = (q0 + lax.broadcasted_iota(jnp.int32, (nsb, TQ), 1)) // SEL_LEN
    valid = j_idx <= cur
    forced = (j_idx == 0) | (j_idx == cur) | (j_idx == cur - 1)
    sub = lax.broadcasted_iota(jnp.int32, (8, TQ), 0)
    nch = nsb // 8
    for g in groups:
        score = jnp.where(valid, imp_t[g] + jnp.where(forced, 1e6, 0.0), -1e9)
        chunks = [score[8 * v:8 * v + 8] for v in range(nch)]
        counts = [jnp.zeros((8, TQ), F32) for _ in range(nch)]
        for i in range(nsb):
            row = jnp.broadcast_to(score[i:i + 1], (8, TQ))
            for v in range(nch):
                if i < 8 * v:
                    ahead = row >= chunks[v]
                elif i >= 8 * v + 8:
                    ahead = row > chunks[v]
                else:
                    ahead = jnp.where(sub > (i - 8 * v),
                                      jnp.where(row >= chunks[v], 1.0, 0.0),
                                      jnp.where(row > chunks[v], 1.0, 0.0)) > 0.5
                counts[v] = counts[v] + jnp.where(ahead, 1.0, 0.0)
        bias_t = jnp.concatenate(
            [jnp.where((c < float(N_SEL)) & valid[8 * v:8 * v + 8], 0.0, NEG) for v, c in enumerate(counts)]
            + [jnp.zeros((LANES - nsb, TQ), F32)], axis=0).astype(BF16)
        rhs_ref[g] = jnp.concatenate([qgs[g], tile4(bias_t)], axis=0)

    m_ref[...] = jnp.full(m_ref.shape, NEG, F32)
    l_ref[...] = jnp.zeros(l_ref.shape, F32)
    acc_ref[...] = jnp.zeros(acc_ref.shape, F32)

    def sel_tile(kt, diagonal):
        k0 = pl.multiple_of(kt * TK_SEL, TK_SEL)
        lhs = jnp.concatenate([ks_ref[0, pl.ds(k0, TK_SEL), :], et_ref[pl.ds(k0, TK_SEL), :]], axis=1)
        vt = vst_ref[0, :, pl.ds(k0, TK_SEL)]
        if diagonal:
            kpos = k0 + lax.broadcasted_iota(jnp.int32, (TK_SEL, TQ), 0)
            tpos = q0 + lax.broadcasted_iota(jnp.int32, (TK_SEL, TQ), 1)
            causal = tile4(kpos <= tpos)
        for g in groups:
            sc = _dot(lhs, rhs_ref[g])
            if diagonal:
                sc = jnp.where(causal, sc, NEG)
            m_i = m_ref[g]
            m_new = jnp.maximum(m_i, _col_reduce(sc, jnp.max))
            alpha = jnp.exp2(m_i - m_new)
            pexp = jnp.exp2(sc - m_new)
            l_ref[g] = alpha * l_ref[g] + _col_reduce(pexp, jnp.sum)
            acc_ref[g] = alpha * acc_ref[g] + _dot(vt, pexp.astype(BF16))
            m_ref[g] = m_new

    last = q0 // TK_SEL

    def sel_body(kt, carry):
        sel_tile(kt, False)
        return carry

    lax.fori_loop(0, last, sel_body, 0)
    sel_tile(last, True)

    k0 = pl.multiple_of(jnp.maximum(q0 - WINDOW, 0), TQ)
    kpos = k0 + lax.broadcasted_iota(jnp.int32, (wkeys, TQ), 0)
    tpos = q0 + lax.broadcasted_iota(jnp.int32, (wkeys, TQ), 1)
    wmask = tile4((kpos <= tpos) & (kpos > tpos - WINDOW))
    kwin = kw_ref[0, pl.ds(k0, wkeys), :]
    vwin = vwt_ref[0, :, pl.ds(k0, wkeys)]
    o_w = []
    for g in groups:
        sc = jnp.where(wmask, _dot(kwin, qgs[g]), NEG)
        pexp = jnp.exp2(sc - _col_reduce(sc, jnp.max))
        o_w.append(_dot(vwin, pexp.astype(BF16)) / _col_reduce(pexp, jnp.sum))

    blocks = []
    for g in groups:
        o_s = acc_ref[g] / l_ref[g]
        for h in range(HPG):
            r0 = LANES * g + 3 * h
            cols = slice(h * TQ, (h + 1) * TQ)
            og = (gt_ref[0, r0:r0 + 1, :] * o_c[g][:, cols] + gt_ref[0, r0 + 1:r0 + 2, :] * o_s[:, cols]
                  + gt_ref[0, r0 + 2:r0 + 3, :] * o_w[g][:, cols])
            blocks.append(og[HEAD_DIM * g:HEAD_DIM * (g + 1)])
    o_ref[0] = jnp.concatenate(blocks, axis=0).T.astype(o_ref.dtype)


def _attention(qt, kc, vct, kv, vt, gt, ovt, et):
    b, _, s = qt.shape
    ncp = kc.shape[1]
    kspec = lambda j: pl.BlockSpec((1, s, LANES), lambda bi, qi: (bi, 0, j))
    vspec = lambda j: pl.BlockSpec((1, LANES, s), lambda bi, qi: (bi, j, 0))
    return pl.pallas_call(
        _attn_kernel,
        grid=(b, s // TQ),
        in_specs=[pl.BlockSpec((1, N_HEADS * LANES, TQ), lambda bi, qi: (bi, 0, qi)),
                  pl.BlockSpec((1, ncp, LANES), lambda bi, qi: (bi, 0, 0)),
                  pl.BlockSpec((1, LANES, ncp), lambda bi, qi: (bi, 0, 0)),
                  kspec(2), vspec(0), kspec(4), vspec(1),
                  pl.BlockSpec((1, N_KV * LANES, TQ), lambda bi, qi: (bi, 0, qi)),
                  pl.BlockSpec(ovt.shape, lambda bi, qi: (0, 0)),
                  pl.BlockSpec(et.shape, lambda bi, qi: (0, 0))],
        out_specs=pl.BlockSpec((1, TQ, N_HEADS * HEAD_DIM), lambda bi, qi: (bi, qi, 0)),
        out_shape=jax.ShapeDtypeStruct((b, s, N_HEADS * HEAD_DIM), BF16),
        scratch_shapes=[pltpu.VMEM((N_KV, 2 * LANES, HPG * TQ), BF16),
                        pltpu.VMEM((N_KV, 1, HPG * TQ), F32), pltpu.VMEM((N_KV, 1, HPG * TQ), F32),
                        pltpu.VMEM((N_KV, LANES, HPG * TQ), F32)],
        compiler_params=_params(("parallel", "arbitrary")),
        name="nsa_attention",
    )(qt, kc, vct, kv, vt, kv, vt, gt, ovt, et)


def _conv_kernel(c_ref, halo_ref, cw_ref, cb_ref, g_ref, b_ref, o_ref, xs_ref, sh_ref):
    i = pl.program_id(1)
    ts = c_ref.shape[1]
    xs_ref[0:CONV_HALO, :] = jnp.where(i > 0, halo_ref[0], 0.0)
    xs_ref[CONV_HALO:CONV_HALO + ts, :] = c_ref[0]
    first = CONV_HALO - (CONV_WIDTH - 1)
    acc = jnp.zeros((ts, CONV_CH), F32)
    for r in range(8):
        offs = [off for off in range(first, first + CONV_WIDTH) if off % 8 == r]
        span = max(offs) - r
        sh_ref[r % 2, 0:span + ts, :] = xs_ref[r:r + span + ts, :]
        for off in offs:
            w = off - first
            acc = acc + sh_ref[r % 2, off - r:off - r + ts, :] * cw_ref[w:w + 1, :]
    y = _layer_norm(acc + cb_ref[...], g_ref[...], b_ref[...])
    o_ref[0] = _silu(y).astype(o_ref.dtype)


def _conv_branch(c, cw, cb, ln_g, ln_b):
    b, s, ch = c.shape
    ts = TS_CONV
    per = ts // CONV_HALO
    vec = pl.BlockSpec((1, ch), lambda bi, i: (0, 0))
    return pl.pallas_call(
        _conv_kernel,
        grid=(b, s // ts),
        in_specs=[pl.BlockSpec((1, ts, ch), lambda bi, i: (bi, i, 0)),
                  pl.BlockSpec((1, CONV_HALO, ch), lambda bi, i: (bi, jnp.maximum(i * per - 1, 0), 0)),
                  pl.BlockSpec((CONV_WIDTH, ch), lambda bi, i: (0, 0)), vec, vec, vec],
        out_specs=pl.BlockSpec((1, ts, ch), lambda bi, i: (bi, i, 0)),
        out_shape=jax.ShapeDtypeStruct((b, s, ch), BF16),
        scratch_shapes=[pltpu.VMEM((CONV_HALO + ts, ch), F32), pltpu.VMEM((2, CONV_HALO + ts, ch), F32)],
        compiler_params=_params(("parallel", "arbitrary")),
        name="conv_branch",
    )(c, c, cw, cb, ln_g, ln_b)


def _outproj_kernel(oa_ref, cc_ref, wt_ref, wb_ref, h_ref, gpost_ref, gnext_ref, h1_ref, u_ref):
    m = _dot(oa_ref[...], wt_ref[...]) + _dot(cc_ref[...], wb_ref[...])
    h1 = h_ref[...] + _rms(m, gpost_ref[...])
    h1_ref[...] = h1
    u_ref[...] = _rms(h1, gnext_ref[...]).astype(u_ref.dtype)


def _outproj(oa, cc, wt, wb, h, gpost, gnext):
    n, d = h.shape
    tm = TM_PROJ
    full = lambda a: pl.BlockSpec(a.shape, lambda i: (0,) * a.ndim)
    row = lambda c: pl.BlockSpec((tm, c), lambda i: (i, 0))
    return pl.pallas_call(
        _outproj_kernel,
        grid=(n // tm,),
        in_specs=[row(oa.shape[1]), row(cc.shape[1]), full(wt), full(wb), row(d), full(gpost), full(gnext)],
        out_specs=[row(d), row(d)],
        out_shape=[jax.ShapeDtypeStruct((n, d), F32), jax.ShapeDtypeStruct((n, d), BF16)],
        compiler_params=_params(("parallel",)),
        name="outproj",
    )(oa, cc, wt, wb, h, gpost, gnext)


def _ffn_kernel(u_ref, wg_ref, wu_ref, wd_ref, h_ref, gpost_ref, gnext_ref, h2_ref, u2_ref, acc_ref):
    f = pl.program_id(1)

    @pl.when(f == 0)
    def _():
        acc_ref[...] = jnp.zeros_like(acc_ref)

    u = u_ref[...]
    act = (_silu(_dot(u, wg_ref[...])) * _dot(u, wu_ref[...])).astype(BF16)
    acc_ref[...] += _dot(act, wd_ref[...])

    @pl.when(f == pl.num_programs(1) - 1)
    def _():
        h2 = h_ref[...] + _rms(acc_ref[...], gpost_ref[...])
        h2_ref[...] = h2
        u2_ref[...] = _rms(h2, gnext_ref[...]).astype(u2_ref.dtype)


def _ffn(u, wg, wu, wd, h, gpost, gnext):
    n, d = h.shape
    ff = wg.shape[1]
    tm, tf = TM_FFN, TF_FFN
    row = pl.BlockSpec((tm, d), lambda i, f: (i, 0))
    vec = pl.BlockSpec((1, d), lambda i, f: (0, 0))
    return pl.pallas_call(
        _ffn_kernel,
        grid=(n // tm, ff // tf),
        in_specs=[row, pl.BlockSpec((d, tf), lambda i, f: (0, f)), pl.BlockSpec((d, tf), lambda i, f: (0, f)),
                  pl.BlockSpec((tf, d), lambda i, f: (f, 0)), row, vec, vec],
        out_specs=[row, row],
        out_shape=[jax.ShapeDtypeStruct((n, d), F32), jax.ShapeDtypeStruct((n, d), BF16)],
        scratch_shapes=[pltpu.VMEM((tm, d), F32)],
        compiler_params=_params(("parallel", "arbitrary")),
        name="dense_ffn",
    )(u, wg, wu, wd, h, gpost, gnext)


def _gmlp_kernel(u_ref, win_ref, lng_ref, lnb_ref, ws_ref, bs_ref, wout_ref, h_ref, gpost_ref, gnext_ref,
                 wrh_ref, wrl_ref, h3_ref, u4_ref, route_ref, mix_ref):
    tm = u_ref.shape[0]
    width = wout_ref.shape[0]
    z = _gelu_tanh(_dot(u_ref[...], win_ref[...]))
    z1 = z[:, :width]
    z2 = _layer_norm(z[:, width:], lng_ref[...], lnb_ref[...]).astype(BF16)
    r_idx = lax.broadcasted_iota(jnp.int32, (GMLP_CHUNK, GMLP_CHUNK), 0)
    c_idx = lax.broadcasted_iota(jnp.int32, (GMLP_CHUNK, GMLP_CHUNK), 1)
    gw = width // GMLP_GROUPS
    for g in range(GMLP_GROUPS):
        wsg = jnp.where(c_idx <= r_idx, ws_ref[g], 0.0).astype(BF16)
        for c in range(tm // GMLP_CHUNK):
            rows = slice(c * GMLP_CHUNK, (c + 1) * GMLP_CHUNK)
            cols = slice(g * gw, (g + 1) * gw)
            mix_ref[rows, cols] = _dot(wsg, z2[rows, cols]) + bs_ref[:, cols]
    gated = (z1 * mix_ref[...]).astype(BF16)
    y = _dot(gated, wout_ref[...])
    h3 = h_ref[...] + _rms(y, gpost_ref[...])
    h3_ref[...] = h3
    u4 = _rms(h3, gnext_ref[...])
    u4_ref[...] = u4

    u_hi = u4.astype(BF16)
    u_lo = (u4 - u_hi.astype(F32)).astype(BF16)
    logits = _dot(u_hi, wrh_ref[...]) + (_dot(u_lo, wrh_ref[...]) + _dot(u_hi, wrl_ref[...]))
    lane = lax.broadcasted_iota(jnp.int32, (tm, LANES), 1).astype(F32)
    lg = jnp.where(lane < float(N_EXPERTS), logits, NEG)
    m1 = jnp.max(lg, axis=-1, keepdims=True)
    i1 = jnp.min(jnp.where(lg == m1, lane, float(LANES)), axis=-1, keepdims=True)
    lg2 = jnp.where(lane == i1, NEG, lg)
    m2 = jnp.max(lg2, axis=-1, keepdims=True)
    i2 = jnp.min(jnp.where(lg2 == m2, lane, float(LANES)), axis=-1, keepdims=True)
    e2 = jnp.exp(m2 - m1)
    den = 1.0 + e2
    route_ref[...] = jnp.where(lane == 0.0, i1, jnp.where(lane == 1.0, i2,
                               jnp.where(lane == 2.0, 1.0 / den, jnp.where(lane == 3.0, e2 / den, 0.0))))


def _gmlp(u, win, lng, lnb, ws, bsb, wout, h, gpost, gnext, wrh, wrl):
    n, d = h.shape
    tm = TM_PROJ
    full = lambda a: pl.BlockSpec(a.shape, lambda i: (0,) * a.ndim)
    row = lambda c: pl.BlockSpec((tm, c), lambda i: (i, 0))
    return pl.pallas_call(
        _gmlp_kernel,
        grid=(n // tm,),
        in_specs=[row(d), full(win), full(lng), full(lnb), full(ws), full(bsb), full(wout), row(d),
                  full(gpost), full(gnext), full(wrh), full(wrl)],
        out_specs=[row(d), row(d), row(LANES)],
        out_shape=[jax.ShapeDtypeStruct((n, d), F32), jax.ShapeDtypeStruct((n, d), F32),
                   jax.ShapeDtypeStruct((n, LANES), F32)],
        scratch_shapes=[pltpu.VMEM((tm, wout.shape[0]), F32)],
        compiler_params=_params(("parallel",)),
        name="gmlp_router",
    )(u, win, lng, lnb, ws, bsb, wout, h, gpost, gnext, wrh, wrl)


def _expert_kernel(tile_e_ref, nused_ref, rowtok_ref, x_hbm, wg_ref, wu_ref, wd_ref, y_ref,
                   xg_ref, xb_ref, acc_ref, sem, *, nf):
    i = pl.program_id(0)
    f = pl.program_id(1)
    tm = xg_ref.shape[1]
    nused = nused_ref[0]
    active = i < nused
    slot = i % 2

    def row_copy(tok, s, r):
        return pltpu.make_async_copy(x_hbm.at[pl.ds(tok, 1)], xg_ref.at[s, pl.ds(r, 1)], sem.at[s])

    def wait_tile(s):
        pltpu.make_async_copy(x_hbm.at[pl.ds(0, tm)], xg_ref.at[s], sem.at[s]).wait()

    @pl.when((i == 0) & (f == 0))
    def _():
        def body(r, carry):
            row_copy(rowtok_ref[r], 0, r).start()
            return carry

        lax.fori_loop(0, tm, body, 0, unroll=8)

    @pl.when((i <= nused) & (f == 0))
    def _():
        wait_tile(slot)

    @pl.when(active & (f == 0))
    def _():
        xb_ref[...] = xg_ref[slot].astype(BF16)
        acc_ref[...] = jnp.zeros_like(acc_ref)

    @pl.when(active)
    def _():
        per = tm // nf
        base = (i + 1) * tm + f * per
        for j in range(per):
            row_copy(rowtok_ref[base + j], 1 - slot, f * per + j).start()
        x = xb_ref[...]
        act = (_silu(_dot(x, wg_ref[0])) * _dot(x, wu_ref[0])).astype(BF16)
        acc_ref[...] += _dot(act, wd_ref[0])

    last = f == pl.num_programs(1) - 1

    @pl.when(active & last)
    def _():
        y_ref[...] = acc_ref[...]

    @pl.when(jnp.logical_not(active) & last)
    def _():
        y_ref[...] = jnp.zeros_like(y_ref)


def _experts(tile_e, nused, rowtok, x, wg, wu, wd):
    n_rows = rowtok.shape[0]
    d = x.shape[1]
    ff = wg.shape[2]
    tm, tf = TM_MOE, TF_MOE
    nf = ff // tf

    def fidx(i, f, te, nu):
        return jnp.where(i < nu[0], f, nf - 1)

    grid_spec = pltpu.PrefetchScalarGridSpec(
        num_scalar_prefetch=3,
        grid=(n_rows // tm, nf),
        in_specs=[pl.BlockSpec(memory_space=pl.ANY),
                  pl.BlockSpec((1, d, tf), lambda i, f, te, nu, rt: (te[i], 0, fidx(i, f, te, nu))),
                  pl.BlockSpec((1, d, tf), lambda i, f, te, nu, rt: (te[i], 0, fidx(i, f, te, nu))),
                  pl.BlockSpec((1, tf, d), lambda i, f, te, nu, rt: (te[i], fidx(i, f, te, nu), 0))],
        out_specs=pl.BlockSpec((tm, d), lambda i, f, te, nu, rt: (i, 0)),
        scratch_shapes=[pltpu.VMEM((2, tm, d), F32), pltpu.VMEM((tm, d), BF16), pltpu.VMEM((tm, d), F32),
                        pltpu.SemaphoreType.DMA((2,))],
    )
    return pl.pallas_call(
        functools.partial(_expert_kernel, nf=nf),
        grid_spec=grid_spec,
        out_shape=jax.ShapeDtypeStruct((n_rows, d), F32),
        compiler_params=_params(("arbitrary", "arbitrary")),
        name="moe_experts",
    )(tile_e, nused, rowtok, x, wg, wu, wd)


def _combine_kernel(dest_ref, y_hbm, route_ref, h_ref, gpost_ref, o_ref, ya_ref, yb_ref, sem):
    i = pl.program_id(0)
    tc = ya_ref.shape[1]
    slot = i % 2

    def copies(d0, d1, s, r):
        return (pltpu.make_async_copy(y_hbm.at[pl.ds(d0, 1)], ya_ref.at[s, pl.ds(r, 1)], sem.at[0, s]),
                pltpu.make_async_copy(y_hbm.at[pl.ds(d1, 1)], yb_ref.at[s, pl.ds(r, 1)], sem.at[1, s]))

    def issue_tile(t, s):
        for r in range(tc):
            p = 2 * (t * tc + r)
            ca, cb = copies(dest_ref[p], dest_ref[p + 1], s, r)
            ca.start()
            cb.start()

    def wait_tile(s):
        pltpu.make_async_copy(y_hbm.at[pl.ds(0, tc)], ya_ref.at[s], sem.at[0, s]).wait()
        pltpu.make_async_copy(y_hbm.at[pl.ds(0, tc)], yb_ref.at[s], sem.at[1, s]).wait()

    @pl.when(i == 0)
    def _():
        issue_tile(0, 0)

    @pl.when(i + 1 < pl.num_programs(0))
    def _():
        issue_tile(i + 1, 1 - slot)

    wait_tile(slot)
    route = route_ref[...]
    moe = route[:, 2:3] * ya_ref[slot] + route[:, 3:4] * yb_ref[slot]
    o_ref[...] = h_ref[...] + _rms(moe, gpost_ref[...])


def _combine(dest, y, route, h, gpost):
    n, d = h.shape
    tc = TC_COMB
    grid_spec = pltpu.PrefetchScalarGridSpec(
        num_scalar_prefetch=1,
        grid=(n // tc,),
        in_specs=[pl.BlockSpec(memory_space=pl.ANY),
                  pl.BlockSpec((tc, LANES), lambda i, ds: (i, 0)),
                  pl.BlockSpec((tc, d), lambda i, ds: (i, 0)),
                  pl.BlockSpec((1, d), lambda i, ds: (0, 0))],
        out_specs=pl.BlockSpec((tc, d), lambda i, ds: (i, 0)),
        scratch_shapes=[pltpu.VMEM((2, tc, d), F32), pltpu.VMEM((2, tc, d), F32),
                        pltpu.SemaphoreType.DMA((2, 2))],
    )
    return pl.pallas_call(
        _combine_kernel,
        grid_spec=grid_spec,
        out_shape=jax.ShapeDtypeStruct((n, d), F32),
        compiler_params=_params(("arbitrary",)),
        name="moe_combine",
    )(dest, y, route, h, gpost)


def _overlap_table_t(seq):
    nc = (seq - CMP_LEN) // CMP_STRIDE + 1
    ncp = seq // CMP_STRIDE
    nsb = seq // SEL_LEN
    cs = np.arange(ncp)[None, :] * CMP_STRIDE
    js = np.arange(LANES)[:, None] * SEL_LEN
    ov = (cs < js + SEL_LEN) & (cs + CMP_LEN > js) & (np.arange(ncp)[None, :] < nc) & (np.arange(LANES)[:, None] < nsb)
    return jnp.asarray(ov.astype(np.float32), BF16)


def _expand_table_t(seq):
    e = (np.arange(seq)[:, None] // SEL_LEN) == np.arange(LANES)[None, :]
    return jnp.asarray(e.astype(np.float32), BF16)


def _layer0_weights(w_in):
    d = w_in.shape[0]
    qc = N_HEADS * HEAD_DIM
    kvc = 6 * N_KV * HEAD_DIM
    gc = 3 * N_HEADS
    gh = N_KV * HEAD_DIM
    wq = (w_in[:, :qc] * (HEAD_DIM ** -0.5 * np.log2(np.e))).reshape(d, N_KV, HPG, HEAD_DIM)
    zeros = jnp.zeros_like(wq[:, 0])
    wq_pad = jnp.stack([jnp.concatenate([wq[:, 0], zeros], axis=-1),
                        jnp.concatenate([zeros, wq[:, 1]], axis=-1)], axis=1)
    wqt = wq_pad.reshape(d, N_HEADS * LANES).T.astype(BF16)
    wkv = w_in[:, qc:qc + kvc]
    wvt = jnp.concatenate([wkv[:, 3 * gh:4 * gh], wkv[:, 5 * gh:6 * gh]], axis=1).T.astype(BF16)
    wg = w_in[:, qc + kvc:qc + kvc + gc].reshape(d, N_KV, 3 * HPG)
    wgt = jnp.pad(wg, ((0, 0), (0, 0), (0, LANES - 3 * HPG))).reshape(d, N_KV * LANES).T.astype(BF16)
    wglu = w_in[:, qc + kvc + gc:].astype(BF16)
    return wqt, wkv.astype(BF16), wvt, wgt, wglu


def _dispatch_plan(route, n_tok):
    tm = TM_MOE
    e_flat = route[:, 0:2].astype(jnp.int32).reshape(-1)
    onehot = (e_flat[:, None] == jnp.arange(N_EXPERTS, dtype=jnp.int32)[None, :]).astype(jnp.int32)
    rank = jnp.sum((jnp.cumsum(onehot, axis=0) - onehot) * onehot, axis=1)
    counts = jnp.sum(onehot, axis=0)
    padded = (counts + tm - 1) // tm * tm
    pad_ends = jnp.cumsum(padded)
    pad_starts = pad_ends - padded
    dest = (pad_starts[e_flat] + rank).astype(jnp.int32)
    n_rows = 2 * n_tok + (N_EXPERTS + 1) * tm
    n_tiles = n_rows // tm
    rowtok = jnp.zeros((n_rows,), jnp.int32).at[dest].set(jnp.arange(2 * n_tok, dtype=jnp.int32) // 2)
    nused = (pad_ends[-1] // tm).astype(jnp.int32)
    tiles = jnp.minimum(jnp.arange(n_tiles, dtype=jnp.int32), nused - 1) * tm
    tile_e = jnp.minimum(jnp.sum((pad_ends[None, :] <= tiles[:, None]).astype(jnp.int32), axis=1), N_EXPERTS - 1)
    return tile_e, nused.reshape(1), rowtok, dest


def kernel(x, norm_mix_pre, norm_mix_post, norm_ffn_pre, norm_ffn_post, nsa_conv_w_in, cmp_pe, cmp_w1, cmp_b1, cmp_w2, cmp_b2, conv_w, conv_b, conv_ln_g, conv_ln_b, nsa_conv_w_out, ffn_w_gate, ffn_w_up, ffn_w_down, gmlp_w_in, gmlp_ln_g, gmlp_ln_b, gmlp_w_s, gmlp_b_s, gmlp_w_out, moe_w_router, moe_w_gate, moe_w_up, moe_w_down):
    b, s, d = x.shape
    n = b * s
    assert d == D_MODEL and s // SEL_LEN == SEL_LEN and s % (2 * TK_SEL) == 0 and n % TM_PROJ == 0
    h0 = x.reshape(n, d)
    vec = lambda a: a.reshape(1, -1).astype(F32)

    wqt, wkv, wvt, wgt, wglu = _layer0_weights(nsa_conv_w_in[0])
    qt, kv, kcv, vt, gt, c = _proj0(h0, vec(norm_mix_pre[0]), wqt, wkv, wvt, wgt, wglu, b)
    kv3 = kv.reshape(b, s, 6 * N_KV * HEAD_DIM)
    cmp = _compress(kcv.reshape(b, s, 2 * LANES), *_compress_weights(cmp_pe[0], cmp_w1[0], cmp_b1[0],
                                                                     cmp_w2[0], cmp_b2[0]))
    kc = cmp[0]
    vct = cmp[1].transpose(0, 2, 1)

    o_attn = _attention(qt, kc, vct, kv3, vt, gt, _overlap_table_t(s), _expand_table_t(s))
    cc = _conv_branch(c.reshape(b, s, CONV_CH), conv_w[0].reshape(CONV_WIDTH, CONV_CH).astype(F32),
                      vec(conv_b[0]), vec(conv_ln_g[0]), vec(conv_ln_b[0]))
    w_out = nsa_conv_w_out[0].astype(BF16)
    nw = N_HEADS * HEAD_DIM
    h1, u1 = _outproj(o_attn.reshape(n, nw), cc.reshape(n, CONV_CH), w_out[:nw], w_out[nw:], h0,
                      vec(norm_mix_post[0]), vec(norm_ffn_pre[0]))
    h2, u2 = _ffn(u1, ffn_w_gate[0].astype(BF16), ffn_w_up[0].astype(BF16), ffn_w_down[0].astype(BF16), h1,
                  vec(norm_ffn_post[0]), vec(norm_mix_pre[1]))

    bsb = jnp.repeat(gmlp_b_s[0].T.astype(F32), d // GMLP_GROUPS, axis=1)
    wr = jnp.pad(moe_w_router[0].astype(F32), ((0, 0), (0, LANES - N_EXPERTS)))
    wr_hi = wr.astype(BF16)
    wr_lo = (wr - wr_hi.astype(F32)).astype(BF16)
    h3, u4, route = _gmlp(u2, gmlp_w_in[0].astype(BF16), vec(gmlp_ln_g[0]), vec(gmlp_ln_b[0]),
                          gmlp_w_s[0].astype(F32), bsb, gmlp_w_out[0].astype(BF16), h2,
                          vec(norm_mix_post[1]), vec(norm_ffn_pre[1]), wr_hi, wr_lo)
    tile_e, nused, rowtok, dest = _dispatch_plan(route, n)
    y = _experts(tile_e, nused, rowtok, u4, moe_w_gate[0].astype(BF16), moe_w_up[0].astype(BF16),
                 moe_w_down[0].astype(BF16))
    h4 = _combine(dest, y, route, h3, vec(norm_ffn_post[1]))
    return h4.reshape(b, s, d)
```

```python
import functools

import numpy as np
import jax
import jax.numpy as jnp
from jax import lax
from jax.experimental import pallas as pl
from jax.experimental.pallas import tpu as pltpu

F32 = jnp.float32
BF16 = jnp.bfloat16

D_MODEL = 1024
N_HEADS = 8
N_KV = 2
HPG = N_HEADS // N_KV
HEAD_DIM = 64
CMP_LEN = 32
CMP_STRIDE = 16
CMP_HIDDEN = 256
SEL_LEN = 64
N_SEL = 16
WINDOW = 512
CONV_CH = D_MODEL // 2
CONV_WIDTH = 31
GMLP_GROUPS = 8
GMLP_CHUNK = 128
N_EXPERTS = 8
EPS = 1e-6
NEG = -1e30

LANES = 128
TQ = 256
TK_SEL = 1024
TM_PROJ = 512
TS_CONV = 512
CONV_HALO = 32
TM_FFN = 256
TF_FFN = 2816
TM_MOE = 512
TF_MOE = 1792
TC_COMB = 256
VMEM_LIMIT = 56 * 1024 * 1024


def _params(sem):
    return pltpu.CompilerParams(dimension_semantics=sem, vmem_limit_bytes=VMEM_LIMIT)


def _dot(a, b):
    return jnp.dot(a, b, preferred_element_type=F32)


def _dot_nt(a, b):
    return lax.dot_general(a, b, (((1,), (1,)), ((), ())), preferred_element_type=F32)


def _rms(x, g):
    return x * lax.rsqrt(jnp.mean(x * x, axis=-1, keepdims=True) + EPS) * g


def _layer_norm(x, g, b):
    mu = jnp.mean(x, axis=-1, keepdims=True)
    xc = x - mu
    var = jnp.mean(xc * xc, axis=-1, keepdims=True)
    return xc * lax.rsqrt(var + EPS) * g + b


def _sigmoid(x):
    return 1.0 / (1.0 + jnp.exp(-x))


def _silu(x):
    return x * _sigmoid(x)


def _gelu_tanh(x):
    c = np.float32(np.sqrt(2.0 / np.pi))
    return 0.5 * x * (1.0 + jnp.tanh(c * (x + 0.044715 * (x * x * x))))


def _proj0_kernel(h_ref, g_ref, wqt_ref, wkv_ref, wvt_ref, wgt_ref, wglu_ref,
                  qt_ref, kv_ref, kcv_ref, vt_ref, gt_ref, c_ref):
    u = _rms(h_ref[...], g_ref[...]).astype(BF16)
    qt_ref[0] = _dot_nt(wqt_ref[...], u).astype(BF16)
    kv = _dot(u, wkv_ref[...])
    kv_ref[...] = kv.astype(BF16)
    kcv_ref[...] = kv[:, :kcv_ref.shape[1]]
    vt_ref[0] = _dot_nt(wvt_ref[...], u).astype(BF16)
    gt_ref[0] = _sigmoid(_dot_nt(wgt_ref[...], u))
    glu = _dot(u, wglu_ref[...])
    c_ref[...] = glu[:, :CONV_CH] * _sigmoid(glu[:, CONV_CH:])


def _proj0(h, g, wqt, wkv, wvt, wgt, wglu, batch):
    n, d = h.shape
    tm = TM_PROJ
    seq = n // batch
    per = seq // tm
    full = lambda a: pl.BlockSpec(a.shape, lambda i: (0,) * a.ndim)
    row = lambda c: pl.BlockSpec((tm, c), lambda i: (i, 0))
    col = lambda r: pl.BlockSpec((1, r, tm), lambda i: (i // per, 0, i % per))
    return pl.pallas_call(
        _proj0_kernel,
        grid=(n // tm,),
        in_specs=[row(d), full(g), full(wqt), full(wkv), full(wvt), full(wgt), full(wglu)],
        out_specs=[col(wqt.shape[0]), row(wkv.shape[1]), row(2 * LANES), col(wvt.shape[0]), col(wgt.shape[0]),
                   row(CONV_CH)],
        out_shape=[jax.ShapeDtypeStruct((batch, wqt.shape[0], seq), BF16),
                   jax.ShapeDtypeStruct((n, wkv.shape[1]), BF16),
                   jax.ShapeDtypeStruct((n, 2 * LANES), F32),
                   jax.ShapeDtypeStruct((batch, wvt.shape[0], seq), BF16),
                   jax.ShapeDtypeStruct((batch, wgt.shape[0], seq), F32),
                   jax.ShapeDtypeStruct((n, CONV_CH), F32)],
        compiler_params=_params(("parallel",)),
        name="proj0",
    )(h, g, wqt, wkv, wvt, wgt, wglu)


def _compress_kernel(x_ref, pe_ref, w1_ref, b1_ref, w2_ref, b2_ref, o_ref):
    nrow = o_ref.shape[2]
    top = jnp.zeros((nrow, w1_ref.shape[3]), F32)
    bot = jnp.zeros((nrow, w1_ref.shape[3]), F32)
    for l in range(CMP_STRIDE):
        x = x_ref[0, pl.ds(l, nrow, stride=CMP_STRIDE), :]
        top = top + _dot((x + pe_ref[0, l]).astype(BF16), w1_ref[0, l])
        bot = bot + _dot((x + pe_ref[0, CMP_STRIDE + l]).astype(BF16), w1_ref[0, CMP_STRIDE + l])
    pre = top + pltpu.roll(bot, nrow - 1, 0) + b1_ref[0]
    hid = _gelu_tanh(pre).astype(BF16)
    o_ref[0, 0] = (_dot(hid, w2_ref[0]) + b2_ref[0]).astype(o_ref.dtype)


def _compress(kcv, pe, w1, b1, w2, b2):
    b, s, _ = kcv.shape
    nr = s // CMP_STRIDE
    return pl.pallas_call(
        _compress_kernel,
        grid=(2, b),
        in_specs=[pl.BlockSpec((1, s, LANES), lambda j, i: (i, 0, j)),
                  pl.BlockSpec((1,) + pe.shape[1:], lambda j, i: (j, 0, 0, 0)),
                  pl.BlockSpec((1,) + w1.shape[1:], lambda j, i: (j, 0, 0, 0)),
                  pl.BlockSpec((1,) + b1.shape[1:], lambda j, i: (j, 0, 0)),
                  pl.BlockSpec((1,) + w2.shape[1:], lambda j, i: (j, 0, 0)),
                  pl.BlockSpec((1,) + b2.shape[1:], lambda j, i: (j, 0, 0))],
        out_specs=pl.BlockSpec((1, 1, nr, LANES), lambda j, i: (j, i, 0, 0)),
        out_shape=jax.ShapeDtypeStruct((2, b, nr, LANES), BF16),
        compiler_params=_params(("parallel", "parallel")),
        name="compress",
    )(kcv, pe, w1, b1, w2, b2)


def _compress_weights(cmp_pe, cmp_w1, cmp_b1, cmp_w2, cmp_b2):
    eye = jnp.eye(N_KV, dtype=F32)
    w1 = cmp_w1.reshape(2, CMP_LEN, HEAD_DIM, CMP_HIDDEN)
    w1 = jnp.einsum('gk,jldh->jlgdkh', eye, w1).reshape(2, CMP_LEN, N_KV * HEAD_DIM, N_KV * CMP_HIDDEN)
    w2 = jnp.einsum('gk,jhd->jghkd', eye, cmp_w2).reshape(2, N_KV * CMP_HIDDEN, N_KV * HEAD_DIM)
    pe = jnp.tile(cmp_pe.reshape(2, CMP_LEN, 1, HEAD_DIM), (1, 1, 1, N_KV)).astype(F32)
    b1 = jnp.tile(cmp_b1.reshape(2, 1, CMP_HIDDEN), (1, 1, N_KV)).astype(F32)
    b2 = jnp.tile(cmp_b2.reshape(2, 1, HEAD_DIM), (1, 1, N_KV)).astype(F32)
    return pe, w1.astype(BF16), b1, w2.astype(BF16), b2


def _col_reduce(x, op):
    rows = x.shape[0]
    part = op(x.reshape(4, rows // 4, x.shape[1]), axis=0)
    return op(part, axis=0, keepdims=True)


def _attn_kernel(qt_ref, kc_ref, vct_ref, ks_ref, vst_ref, kw_ref, vwt_ref, gt_ref,
                 ovt_ref, et_ref, o_ref, rhs_ref, m_ref, l_ref, acc_ref):
    qi = pl.program_id(1)
    q0 = qi * TQ
    nsb = SEL_LEN
    ncp = kc_ref.shape[1]
    wkeys = WINDOW + TQ
    groups = range(N_KV)
    tile4 = lambda a: jnp.concatenate([a] * HPG, axis=1)
    qgs = [jnp.concatenate(
        [qt_ref[0, LANES * (HPG * g + h):LANES * (HPG * g + h + 1), :] for h in range(HPG)], axis=1)
        for g in groups]

    n_idx = lax.broadcasted_iota(jnp.int32, (ncp, TQ), 0)
    t_idx = q0 + lax.broadcasted_iota(jnp.int32, (ncp, TQ), 1)
    cmask = tile4(n_idx * CMP_STRIDE + (CMP_LEN - 1) <= t_idx)
    o_c, imp_t = [], []
    for g in groups:
        s = jnp.where(cmask, _dot(kc_ref[0], qgs[g]), NEG)
        m = _col_reduce(s, jnp.max)
        p = jnp.where(cmask, jnp.exp2(s - m), 0.0)
        l = _col_reduce(p, jnp.sum)
        p = p / jnp.where(l > 0.0, l, 1.0)
        o_c.append(_dot(vct_ref[0], p.astype(BF16)))
        psum = p[:, 0:TQ] + p[:, TQ:2 * TQ] + p[:, 2 * TQ:3 * TQ] + p[:, 3 * TQ:4 * TQ]
        imp_t.append(_dot(ovt_ref[...], psum.astype(BF16))[:nsb])

    j_idx = lax.broadcasted_iota(jnp.int32, (nsb, TQ), 0)
    cur = (q0 + lax.broadcasted_iota(jnp.int32, (nsb, TQ), 1)) // SEL_LEN
    valid = j_idx <= cur
    forced = (j_idx == 0) | (j_idx == cur) | (j_idx == cur - 1)
    sub = lax.broadcasted_iota(jnp.int32, (8, TQ), 0)
    nch = nsb // 8
    for g in groups:
        score = jnp.where(valid, imp_t[g] + jnp.where(forced, 1e6, 0.0), -1e9)
        chunks = [score[8 * v:8 * v + 8] for v in range(nch)]
        counts = [jnp.zeros((8, TQ), F32) for _ in range(nch)]
        for i in range(nsb):
            row = jnp.broadcast_to(score[i:i + 1], (8, TQ))
            for v in range(nch):
                if i < 8 * v:
                    ahead = row >= chunks[v]
                elif i >= 8 * v + 8:
                    ahead = row > chunks[v]
                else:
                    ahead = jnp.where(sub > (i - 8 * v),
                                      jnp.where(row >= chunks[v], 1.0, 0.0),
                                      jnp.where(row > chunks[v], 1.0, 0.0)) > 0.5
                counts[v] = counts[v] + jnp.where(ahead, 1.0, 0.0)
        bias_t = jnp.concatenate(
            [jnp.where((c < float(N_SEL)) & valid[8 * v:8 * v + 8], 0.0, NEG) for v, c in enumerate(counts)]
            + [jnp.zeros((LANES - nsb, TQ), F32)], axis=0).astype(BF16)
        rhs_ref[g] = jnp.concatenate([qgs[g], tile4(bias_t)], axis=0)

    m_ref[...] = jnp.full(m_ref.shape, NEG, F32)
    l_ref[...] = jnp.zeros(l_ref.shape, F32)
    acc_ref[...] = jnp.zeros(acc_ref.shape, F32)

    def sel_tile(kt, diagonal):
        k0 = pl.multiple_of(kt * TK_SEL, TK_SEL)
        lhs = jnp.concatenate([ks_ref[0, pl.ds(k0, TK_SEL), :], et_ref[pl.ds(k0, TK_SEL), :]], axis=1)
        vt = vst_ref[0, :, pl.ds(k0, TK_SEL)]
        if diagonal:
            kpos = k0 + lax.broadcasted_iota(jnp.int32, (TK_SEL, TQ), 0)
            tpos = q0 + lax.broadcasted_iota(jnp.int32, (TK_SEL, TQ), 1)
            causal = tile4(kpos <= tpos)
        for g in groups:
            sc = _dot(lhs, rhs_ref[g])
            if diagonal:
                sc = jnp.where(causal, sc, NEG)
            m_i = m_ref[g]
            m_new = jnp.maximum(m_i, _col_reduce(sc, jnp.max))
            alpha = jnp.exp2(m_i - m_new)
            pexp = jnp.exp2(sc - m_new)
            l_ref[g] = alpha * l_ref[g] + _col_reduce(pexp, jnp.sum)
            acc_ref[g] = alpha * acc_ref[g] + _dot(vt, pexp.astype(BF16))
            m_ref[g] = m_new

    last = q0 // TK_SEL

    def sel_body(kt, carry):
        sel_tile(kt, False)
        return carry

    lax.fori_loop(0, last, sel_body, 0)
    sel_tile(last, True)

    k0 = pl.multiple_of(jnp.maximum(q0 - WINDOW, 0), TQ)
    kpos = k0 + lax.broadcasted_iota(jnp.int32, (wkeys, TQ), 0)
    tpos = q0 + lax.broadcasted_iota(jnp.int32, (wkeys, TQ), 1)
    wmask = tile4((kpos <= tpos) & (kpos > tpos - WINDOW))
    kwin = kw_ref[0, pl.ds(k0, wkeys), :]
    vwin = vwt_ref[0, :, pl.ds(k0, wkeys)]
    o_w = []
    for g in groups:
        sc = jnp.where(wmask, _dot(kwin, qgs[g]), NEG)
        pexp = jnp.exp2(sc - _col_reduce(sc, jnp.max))
        o_w.append(_dot(vwin, pexp.astype(BF16)) / _col_reduce(pexp, jnp.sum))

    blocks = []
    for g in groups:
        o_s = acc_ref[g] / l_ref[g]
        for h in range(HPG):
            r0 = LANES * g + 3 * h
            cols = slice(h * TQ, (h + 1) * TQ)
            og = (gt_ref[0, r0:r0 + 1, :] * o_c[g][:, cols] + gt_ref[0, r0 + 1:r0 + 2, :] * o_s[:, cols]
                  + gt_ref[0, r0 + 2:r0 + 3, :] * o_w[g][:, cols])
            blocks.append(og[HEAD_DIM * g:HEAD_DIM * (g + 1)])
    o_ref[0] = jnp.concatenate(blocks, axis=0).T.astype(o_ref.dtype)


def _attention(qt, kc, vct, kv, vt, gt, ovt, et):
    b, _, s = qt.shape
    ncp = kc.shape[1]
    kspec = lambda j: pl.BlockSpec((1, s, LANES), lambda bi, qi: (bi, 0, j))
    vspec = lambda j: pl.BlockSpec((1, LANES, s), lambda bi, qi: (bi, j, 0))
    return pl.pallas_call(
        _attn_kernel,
        grid=(b, s // TQ),
        in_specs=[pl.BlockSpec((1, N_HEADS * LANES, TQ), lambda bi, qi: (bi, 0, qi)),
                  pl.BlockSpec((1, ncp, LANES), lambda bi, qi: (bi, 0, 0)),
                  pl.BlockSpec((1, LANES, ncp), lambda bi, qi: (bi, 0, 0)),
                  kspec(2), vspec(0), kspec(4), vspec(1),
                  pl.BlockSpec((1, N_KV * LANES, TQ), lambda bi, qi: (bi, 0, qi)),
                  pl.BlockSpec(ovt.shape, lambda bi, qi: (0, 0)),
                  pl.BlockSpec(et.shape, lambda bi, qi: (0, 0))],
        out_specs=pl.BlockSpec((1, TQ, N_HEADS * HEAD_DIM), lambda bi, qi: (bi, qi, 0)),
        out_shape=jax.ShapeDtypeStruct((b, s, N_HEADS * HEAD_DIM), BF16),
        scratch_shapes=[pltpu.VMEM((N_KV, 2 * LANES, HPG * TQ), BF16),
                        pltpu.VMEM((N_KV, 1, HPG * TQ), F32), pltpu.VMEM((N_KV, 1, HPG * TQ), F32),
                        pltpu.VMEM((N_KV, LANES, HPG * TQ), F32)],
        compiler_params=_params(("parallel", "arbitrary")),
        name="nsa_attention",
    )(qt, kc, vct, kv, vt, kv, vt, gt, ovt, et)


def _conv_kernel(c_ref, halo_ref, cw_ref, cb_ref, g_ref, b_ref, o_ref, xs_ref, sh_ref):
    i = pl.program_id(1)
    ts = c_ref.shape[1]
    xs_ref[0:CONV_HALO, :] = jnp.where(i > 0, halo_ref[0], 0.0)
    xs_ref[CONV_HALO:CONV_HALO + ts, :] = c_ref[0]
    first = CONV_HALO - (CONV_WIDTH - 1)
    acc = jnp.zeros((ts, CONV_CH), F32)
    for r in range(8):
        offs = [off for off in range(first, first + CONV_WIDTH) if off % 8 == r]
        span = max(offs) - r
        sh_ref[r % 2, 0:span + ts, :] = xs_ref[r:r + span + ts, :]
        for off in offs:
            w = off - first
            acc = acc + sh_ref[r % 2, off - r:off - r + ts, :] * cw_ref[w:w + 1, :]
    y = _layer_norm(acc + cb_ref[...], g_ref[...], b_ref[...])
    o_ref[0] = _silu(y).astype(o_ref.dtype)


def _conv_branch(c, cw, cb, ln_g, ln_b):
    b, s, ch = c.shape
    ts = TS_CONV
    per = ts // CONV_HALO
    vec = pl.BlockSpec((1, ch), lambda bi, i: (0, 0))
    return pl.pallas_call(
        _conv_kernel,
        grid=(b, s // ts),
        in_specs=[pl.BlockSpec((1, ts, ch), lambda bi, i: (bi, i, 0)),
                  pl.BlockSpec((1, CONV_HALO, ch), lambda bi, i: (bi, jnp.maximum(i * per - 1, 0), 0)),
                  pl.BlockSpec((CONV_WIDTH, ch), lambda bi, i: (0, 0)), vec, vec, vec],
        out_specs=pl.BlockSpec((1, ts, ch), lambda bi, i: (bi, i, 0)),
        out_shape=jax.ShapeDtypeStruct((b, s, ch), BF16),
        scratch_shapes=[pltpu.VMEM((CONV_HALO + ts, ch), F32), pltpu.VMEM((2, CONV_HALO + ts, ch), F32)],
        compiler_params=_params(("parallel", "arbitrary")),
        name="conv_branch",
    )(c, c, cw, cb, ln_g, ln_b)


def _outproj_kernel(oa_ref, cc_ref, wt_ref, wb_ref, h_ref, gpost_ref, gnext_ref, h1_ref, u_ref):
    m = _dot(oa_ref[...], wt_ref[...]) + _dot(cc_ref[...], wb_ref[...])
    h1 = h_ref[...] + _rms(m, gpost_ref[...])
    h1_ref[...] = h1
    u_ref[...] = _rms(h1, gnext_ref[...]).astype(u_ref.dtype)


def _outproj(oa, cc, wt, wb, h, gpost, gnext):
    n, d = h.shape
    tm = TM_PROJ
    full = lambda a: pl.BlockSpec(a.shape, lambda i: (0,) * a.ndim)
    row = lambda c: pl.BlockSpec((tm, c), lambda i: (i, 0))
    return pl.pallas_call(
        _outproj_kernel,
        grid=(n // tm,),
        in_specs=[row(oa.shape[1]), row(cc.shape[1]), full(wt), full(wb), row(d), full(gpost), full(gnext)],
        out_specs=[row(d), row(d)],
        out_shape=[jax.ShapeDtypeStruct((n, d), F32), jax.ShapeDtypeStruct((n, d), BF16)],
        compiler_params=_params(("parallel",)),
        name="outproj",
    )(oa, cc, wt, wb, h, gpost, gnext)


def _ffn_kernel(u_ref, wg_ref, wu_ref, wd_ref, h_ref, gpost_ref, gnext_ref, h2_ref, u2_ref, acc_ref):
    f = pl.program_id(1)

    @pl.when(f == 0)
    def _():
        acc_ref[...] = jnp.zeros_like(acc_ref)

    u = u_ref[...]
    act = (_silu(_dot(u, wg_ref[...])) * _dot(u, wu_ref[...])).astype(BF16)
    acc_ref[...] += _dot(act, wd_ref[...])

    @pl.when(f == pl.num_programs(1) - 1)
    def _():
        h2 = h_ref[...] + _rms(acc_ref[...], gpost_ref[...])
        h2_ref[...] = h2
        u2_ref[...] = _rms(h2, gnext_ref[...]).astype(u2_ref.dtype)


def _ffn(u, wg, wu, wd, h, gpost, gnext):
    n, d = h.shape
    ff = wg.shape[1]
    tm, tf = TM_FFN, TF_FFN
    row = pl.BlockSpec((tm, d), lambda i, f: (i, 0))
    vec = pl.BlockSpec((1, d), lambda i, f: (0, 0))
    return pl.pallas_call(
        _ffn_kernel,
        grid=(n // tm, ff // tf),
        in_specs=[row, pl.BlockSpec((d, tf), lambda i, f: (0, f)), pl.BlockSpec((d, tf), lambda i, f: (0, f)),
                  pl.BlockSpec((tf, d), lambda i, f: (f, 0)), row, vec, vec],
        out_specs=[row, row],
        out_shape=[jax.ShapeDtypeStruct((n, d), F32), jax.ShapeDtypeStruct((n, d), BF16)],
        scratch_shapes=[pltpu.VMEM((tm, d), F32)],
        compiler_params=_params(("parallel", "arbitrary")),
        name="dense_ffn",
    )(u, wg, wu, wd, h, gpost, gnext)


def _gmlp_kernel(u_ref, win_ref, lng_ref, lnb_ref, ws_ref, bs_ref, wout_ref, h_ref, gpost_ref, gnext_ref,
                 wrh_ref, wrl_ref, h3_ref, u4_ref, route_ref, mix_ref):
    tm = u_ref.shape[0]
    width = wout_ref.shape[0]
    z = _gelu_tanh(_dot(u_ref[...], win_ref[...]))
    z1 = z[:, :width]
    z2 = _layer_norm(z[:, width:], lng_ref[...], lnb_ref[...]).astype(BF16)
    r_idx = lax.broadcasted_iota(jnp.int32, (GMLP_CHUNK, GMLP_CHUNK), 0)
    c_idx = lax.broadcasted_iota(jnp.int32, (GMLP_CHUNK, GMLP_CHUNK), 1)
    gw = width // GMLP_GROUPS
    for g in range(GMLP_GROUPS):
        wsg = jnp.where(c_idx <= r_idx, ws_ref[g], 0.0).astype(BF16)
        for c in range(tm // GMLP_CHUNK):
            rows = slice(c * GMLP_CHUNK, (c + 1) * GMLP_CHUNK)
            cols = slice(g * gw, (g + 1) * gw)
            mix_ref[rows, cols] = _dot(wsg, z2[rows, cols]) + bs_ref[:, cols]
    gated = (z1 * mix_ref[...]).astype(BF16)
    y = _dot(gated, wout_ref[...])
    h3 = h_ref[...] + _rms(y, gpost_ref[...])
    h3_ref[...] = h3
    u4 = _rms(h3, gnext_ref[...])
    u4_ref[...] = u4

    u_hi = u4.astype(BF16)
    u_lo = (u4 - u_hi.astype(F32)).astype(BF16)
    logits = _dot(u_hi, wrh_ref[...]) + (_dot(u_lo, wrh_ref[...]) + _dot(u_hi, wrl_ref[...]))
    lane = lax.broadcasted_iota(jnp.int32, (tm, LANES), 1).astype(F32)
    lg = jnp.where(lane < float(N_EXPERTS), logits, NEG)
    m1 = jnp.max(lg, axis=-1, keepdims=True)
    i1 = jnp.min(jnp.where(lg == m1, lane, float(LANES)), axis=-1, keepdims=True)
    lg2 = jnp.where(lane == i1, NEG, lg)
    m2 = jnp.max(lg2, axis=-1, keepdims=True)
    i2 = jnp.min(jnp.where(lg2 == m2, lane, float(LANES)), axis=-1, keepdims=True)
    e2 = jnp.exp(m2 - m1)
    den = 1.0 + e2
    route_ref[...] = jnp.where(lane == 0.0, i1, jnp.where(lane == 1.0, i2,
                               jnp.where(lane == 2.0, 1.0 / den, jnp.where(lane == 3.0, e2 / den, 0.0))))


def _gmlp(u, win, lng, lnb, ws, bsb, wout, h, gpost, gnext, wrh, wrl):
    n, d = h.shape
    tm = TM_PROJ
    full = lambda a: pl.BlockSpec(a.shape, lambda i: (0,) * a.ndim)
    row = lambda c: pl.BlockSpec((tm, c), lambda i: (i, 0))
    return pl.pallas_call(
        _gmlp_kernel,
        grid=(n // tm,),
        in_specs=[row(d), full(win), full(lng), full(lnb), full(ws), full(bsb), full(wout), row(d),
                  full(gpost), full(gnext), full(wrh), full(wrl)],
        out_specs=[row(d), row(d), row(LANES)],
        out_shape=[jax.ShapeDtypeStruct((n, d), F32), jax.ShapeDtypeStruct((n, d), F32),
                   jax.ShapeDtypeStruct((n, LANES), F32)],
        scratch_shapes=[pltpu.VMEM((tm, wout.shape[0]), F32)],
        compiler_params=_params(("parallel",)),
        name="gmlp_router",
    )(u, win, lng, lnb, ws, bsb, wout, h, gpost, gnext, wrh, wrl)


def _expert_kernel(tile_e_ref, nused_ref, rowtok_ref, x_hbm, wg_ref, wu_ref, wd_ref, y_ref,
                   xg_ref, xb_ref, acc_ref, sem, *, nf):
    i = pl.program_id(0)
    f = pl.program_id(1)
    tm = xg_ref.shape[1]
    nused = nused_ref[0]
    active = i < nused
    slot = i % 2

    def row_copy(tok, s, r):
        return pltpu.make_async_copy(x_hbm.at[pl.ds(tok, 1)], xg_ref.at[s, pl.ds(r, 1)], sem.at[s])

    def wait_tile(s):
        pltpu.make_async_copy(x_hbm.at[pl.ds(0, tm)], xg_ref.at[s], sem.at[s]).wait()

    @pl.when((i == 0) & (f == 0))
    def _():
        def body(r, carry):
            row_copy(rowtok_ref[r], 0, r).start()
            return carry

        lax.fori_loop(0, tm, body, 0, unroll=8)

    @pl.when((i <= nused) & (f == 0))
    def _():
        wait_tile(slot)

    @pl.when(active & (f == 0))
    def _():
        xb_ref[...] = xg_ref[slot].astype(BF16)
        acc_ref[...] = jnp.zeros_like(acc_ref)

    @pl.when(active)
    def _():
        per = tm // nf
        base = (i + 1) * tm + f * per
        for j in range(per):
            row_copy(rowtok_ref[base + j], 1 - slot, f * per + j).start(priority=j % 2)
        x = xb_ref[...]
        act = (_silu(_dot(x, wg_ref[0])) * _dot(x, wu_ref[0])).astype(BF16)
        acc_ref[...] += _dot(act, wd_ref[0])

    last = f == pl.num_programs(1) - 1

    @pl.when(active & last)
    def _():
        y_ref[...] = acc_ref[...]

    @pl.when(jnp.logical_not(active) & last)
    def _():
        y_ref[...] = jnp.zeros_like(y_ref)


def _experts(tile_e, nused, rowtok, x, wg, wu, wd):
    n_rows = rowtok.shape[0]
    d = x.shape[1]
    ff = wg.shape[2]
    tm, tf = TM_MOE, TF_MOE
    nf = ff // tf

    def fidx(i, f, te, nu):
        return jnp.where(i < nu[0], f, nf - 1)

    grid_spec = pltpu.PrefetchScalarGridSpec(
        num_scalar_prefetch=3,
        grid=(n_rows // tm, nf),
        in_specs=[pl.BlockSpec(memory_space=pl.ANY),
                  pl.BlockSpec((1, d, tf), lambda i, f, te, nu, rt: (te[i], 0, fidx(i, f, te, nu))),
                  pl.BlockSpec((1, d, tf), lambda i, f, te, nu, rt: (te[i], 0, fidx(i, f, te, nu))),
                  pl.BlockSpec((1, tf, d), lambda i, f, te, nu, rt: (te[i], fidx(i, f, te, nu), 0))],
        out_specs=pl.BlockSpec((tm, d), lambda i, f, te, nu, rt: (i, 0)),
        scratch_shapes=[pltpu.VMEM((2, tm, d), F32), pltpu.VMEM((tm, d), BF16), pltpu.VMEM((tm, d), F32),
                        pltpu.SemaphoreType.DMA((2,))],
    )
    return pl.pallas_call(
        functools.partial(_expert_kernel, nf=nf),
        grid_spec=grid_spec,
        out_shape=jax.ShapeDtypeStruct((n_rows, d), F32),
        compiler_params=_params(("arbitrary", "arbitrary")),
        name="moe_experts",
    )(tile_e, nused, rowtok, x, wg, wu, wd)


def _combine_kernel(dest_ref, y_hbm, route_ref, h_ref, gpost_ref, o_ref, ya_ref, yb_ref, sem):
    i = pl.program_id(0)
    tc = ya_ref.shape[1]
    slot = i % 2

    def copies(d0, d1, s, r):
        return (pltpu.make_async_copy(y_hbm.at[pl.ds(d0, 1)], ya_ref.at[s, pl.ds(r, 1)], sem.at[0, s]),
                pltpu.make_async_copy(y_hbm.at[pl.ds(d1, 1)], yb_ref.at[s, pl.ds(r, 1)], sem.at[1, s]))

    def issue_tile(t, s):
        for r in range(tc):
            p = 2 * (t * tc + r)
            ca, cb = copies(dest_ref[p], dest_ref[p + 1], s, r)
            ca.start(priority=0)
            cb.start(priority=1)

    def wait_tile(s):
        pltpu.make_async_copy(y_hbm.at[pl.ds(0, tc)], ya_ref.at[s], sem.at[0, s]).wait()
        pltpu.make_async_copy(y_hbm.at[pl.ds(0, tc)], yb_ref.at[s], sem.at[1, s]).wait()

    @pl.when(i == 0)
    def _():
        issue_tile(0, 0)

    @pl.when(i + 1 < pl.num_programs(0))
    def _():
        issue_tile(i + 1, 1 - slot)

    wait_tile(slot)
    route = route_ref[...]
    moe = route[:, 2:3] * ya_ref[slot] + route[:, 3:4] * yb_ref[slot]
    o_ref[...] = h_ref[...] + _rms(moe, gpost_ref[...])


def _combine(dest, y, route, h, gpost):
    n, d = h.shape
    tc = TC_COMB
    grid_spec = pltpu.PrefetchScalarGridSpec(
        num_scalar_prefetch=1,
        grid=(n // tc,),
        in_specs=[pl.BlockSpec(memory_space=pl.ANY),
                  pl.BlockSpec((tc, LANES), lambda i, ds: (i, 0)),
                  pl.BlockSpec((tc, d), lambda i, ds: (i, 0)),
                  pl.BlockSpec((1, d), lambda i, ds: (0, 0))],
        out_specs=pl.BlockSpec((tc, d), lambda i, ds: (i, 0)),
        scratch_shapes=[pltpu.VMEM((2, tc, d), F32), pltpu.VMEM((2, tc, d), F32),
                        pltpu.SemaphoreType.DMA((2, 2))],
    )
    return pl.pallas_call(
        _combine_kernel,
        grid_spec=grid_spec,
        out_shape=jax.ShapeDtypeStruct((n, d), F32),
        compiler_params=_params(("arbitrary",)),
        name="moe_combine",
    )(dest, y, route, h, gpost)


def _overlap_table_t(seq):
    nc = (seq - CMP_LEN) // CMP_STRIDE + 1
    ncp = seq // CMP_STRIDE
    nsb = seq // SEL_LEN
    cs = np.arange(ncp)[None, :] * CMP_STRIDE
    js = np.arange(LANES)[:, None] * SEL_LEN
    ov = (cs < js + SEL_LEN) & (cs + CMP_LEN > js) & (np.arange(ncp)[None, :] < nc) & (np.arange(LANES)[:, None] < nsb)
    return jnp.asarray(ov.astype(np.float32), BF16)


def _expand_table_t(seq):
    e = (np.arange(seq)[:, None] // SEL_LEN) == np.arange(LANES)[None, :]
    return jnp.asarray(e.astype(np.float32), BF16)


def _layer0_weights(w_in):
    d = w_in.shape[0]
    qc = N_HEADS * HEAD_DIM
    kvc = 6 * N_KV * HEAD_DIM
    gc = 3 * N_HEADS
    gh = N_KV * HEAD_DIM
    wq = (w_in[:, :qc] * (HEAD_DIM ** -0.5 * np.log2(np.e))).reshape(d, N_KV, HPG, HEAD_DIM)
    zeros = jnp.zeros_like(wq[:, 0])
    wq_pad = jnp.stack([jnp.concatenate([wq[:, 0], zeros], axis=-1),
                        jnp.concatenate([zeros, wq[:, 1]], axis=-1)], axis=1)
    wqt = wq_pad.reshape(d, N_HEADS * LANES).T.astype(BF16)
    wkv = w_in[:, qc:qc + kvc]
    wvt = jnp.concatenate([wkv[:, 3 * gh:4 * gh], wkv[:, 5 * gh:6 * gh]], axis=1).T.astype(BF16)
    wg = w_in[:, qc + kvc:qc + kvc + gc].reshape(d, N_KV, 3 * HPG)
    wgt = jnp.pad(wg, ((0, 0), (0, 0), (0, LANES - 3 * HPG))).reshape(d, N_KV * LANES).T.astype(BF16)
    wglu = w_in[:, qc + kvc + gc:].astype(BF16)
    return wqt, wkv.astype(BF16), wvt, wgt, wglu


def _dispatch_plan(route, n_tok):
    tm = TM_MOE
    e_flat = route[:, 0:2].astype(jnp.int32).reshape(-1)
    onehot = (e_flat[:, None] == jnp.arange(N_EXPERTS, dtype=jnp.int32)[None, :]).astype(jnp.int32)
    rank = jnp.sum((jnp.cumsum(onehot, axis=0) - onehot) * onehot, axis=1)
    counts = jnp.sum(onehot, axis=0)
    padded = (counts + tm - 1) // tm * tm
    pad_ends = jnp.cumsum(padded)
    pad_starts = pad_ends - padded
    dest = (pad_starts[e_flat] + rank).astype(jnp.int32)
    n_rows = 2 * n_tok + (N_EXPERTS + 1) * tm
    n_tiles = n_rows // tm
    rowtok = jnp.zeros((n_rows,), jnp.int32).at[dest].set(jnp.arange(2 * n_tok, dtype=jnp.int32) // 2)
    nused = (pad_ends[-1] // tm).astype(jnp.int32)
    tiles = jnp.minimum(jnp.arange(n_tiles, dtype=jnp.int32), nused - 1) * tm
    tile_e = jnp.minimum(jnp.sum((pad_ends[None, :] <= tiles[:, None]).astype(jnp.int32), axis=1), N_EXPERTS - 1)
    return tile_e, nused.reshape(1), rowtok, dest


def kernel(x, norm_mix_pre, norm_mix_post, norm_ffn_pre, norm_ffn_post, nsa_conv_w_in, cmp_pe, cmp_w1, cmp_b1, cmp_w2, cmp_b2, conv_w, conv_b, conv_ln_g, conv_ln_b, nsa_conv_w_out, ffn_w_gate, ffn_w_up, ffn_w_down, gmlp_w_in, gmlp_ln_g, gmlp_ln_b, gmlp_w_s, gmlp_b_s, gmlp_w_out, moe_w_router, moe_w_gate, moe_w_up, moe_w_down):
    b, s, d = x.shape
    n = b * s
    assert d == D_MODEL and s // SEL_LEN == SEL_LEN and s % (2 * TK_SEL) == 0 and n % TM_PROJ == 0
    h0 = x.reshape(n, d)
    vec = lambda a: a.reshape(1, -1).astype(F32)

    wqt, wkv, wvt, wgt, wglu = _layer0_weights(nsa_conv_w_in[0])
    qt, kv, kcv, vt, gt, c = _proj0(h0, vec(norm_mix_pre[0]), wqt, wkv, wvt, wgt, wglu, b)
    kv3 = kv.reshape(b, s, 6 * N_KV * HEAD_DIM)
    cmp = _compress(kcv.reshape(b, s, 2 * LANES), *_compress_weights(cmp_pe[0], cmp_w1[0], cmp_b1[0],
                                                                     cmp_w2[0], cmp_b2[0]))
    kc = cmp[0]
    vct = cmp[1].transpose(0, 2, 1)

    o_attn = _attention(qt, kc, vct, kv3, vt, gt, _overlap_table_t(s), _expand_table_t(s))
    cc = _conv_branch(c.reshape(b, s, CONV_CH), conv_w[0].reshape(CONV_WIDTH, CONV_CH).astype(F32),
                      vec(conv_b[0]), vec(conv_ln_g[0]), vec(conv_ln_b[0]))
    w_out = nsa_conv_w_out[0].astype(BF16)
    nw = N_HEADS * HEAD_DIM
    h1, u1 = _outproj(o_attn.reshape(n, nw), cc.reshape(n, CONV_CH), w_out[:nw], w_out[nw:], h0,
                      vec(norm_mix_post[0]), vec(norm_ffn_pre[0]))
    h2, u2 = _ffn(u1, ffn_w_gate[0].astype(BF16), ffn_w_up[0].astype(BF16), ffn_w_down[0].astype(BF16), h1,
                  vec(norm_ffn_post[0]), vec(norm_mix_pre[1]))

    bsb = jnp.repeat(gmlp_b_s[0].T.astype(F32), d // GMLP_GROUPS, axis=1)
    wr = jnp.pad(moe_w_router[0].astype(F32), ((0, 0), (0, LANES - N_EXPERTS)))
    wr_hi = wr.astype(BF16)
    wr_lo = (wr - wr_hi.astype(F32)).astype(BF16)
    h3, u4, route = _gmlp(u2, gmlp_w_in[0].astype(BF16), vec(gmlp_ln_g[0]), vec(gmlp_ln_b[0]),
                          gmlp_w_s[0].astype(F32), bsb, gmlp_w_out[0].astype(BF16), h2,
                          vec(norm_mix_post[1]), vec(norm_ffn_pre[1]), wr_hi, wr_lo)
    tile_e, nused, rowtok, dest = _dispatch_plan(route, n)
    y = _experts(tile_e, nused, rowtok, u4, moe_w_gate[0].astype(BF16), moe_w_up[0].astype(BF16),
                 moe_w_down[0].astype(BF16))
    h4 = _combine(dest, y, route, h3, vec(norm_ffn_post[1]))
    return h4.reshape(b, s, d)
```
